```python
import math
import jax, jax.numpy as jnp
from jax import lax
import numpy as np

D_MODEL = 2048
BATCH = 2
SEQ = 8192
DEPTH = 4

HEAD_DIM = 128
A_Q_HEADS = 8
A_KV_HEADS = 2
A_HALF_WINDOW = 128
B_HEADS = 4
B_HEAD_DIM = 256
B_CONV = 5
B_CHUNK = 64
C_HEADS = D_MODEL // HEAD_DIM
C_PATTERNS = ((128, 1), (512, 4), (2048, 16))
MEM_LEN = 256
X_HEADS = 4
FFN_DIM = 7168
N_EXPERTS = 8
TOP_K = 2
ROPE_THETA = 500000.0
ROT_DIM = HEAD_DIM // 4
EPS = 1e-6

A_Q_W = A_Q_HEADS * HEAD_DIM
A_KV_W = A_KV_HEADS * HEAD_DIM
B_W = B_HEADS * B_HEAD_DIM
C_W = C_HEADS * HEAD_DIM
X_W = X_HEADS * HEAD_DIM
EVEN_SPLITS = (A_Q_W, A_Q_W + A_KV_W, A_Q_W + 2 * A_KV_W, A_Q_W + 2 * A_KV_W + 2 * B_W,
               A_Q_W + 2 * A_KV_W + 3 * B_W, A_Q_W + 2 * A_KV_W + 4 * B_W)
EVEN_IN = A_Q_W + 2 * A_KV_W + 4 * B_W + 4 * B_HEADS
EVEN_OUT = A_Q_W + B_W

kernel_name = "hybrid_swa_mlstm_dilated_moe_encoder"


def rmsnorm(x, g):
    xf = x.astype(jnp.float32)
    y = xf * lax.rsqrt(jnp.mean(xf * xf, axis=-1, keepdims=True) + EPS)
    return (y * g.astype(jnp.float32)).astype(x.dtype)


def rope_tables(n):
    pos = jnp.arange(n, dtype=jnp.float32)
    inv = jnp.power(ROPE_THETA, -jnp.arange(0, ROT_DIM, 2, dtype=jnp.float32) / ROT_DIM)
    ang = pos[:, None] * inv[None, :]
    return jnp.cos(ang), jnp.sin(ang)


def partial_rope(x, cos, sin):
    half = ROT_DIM // 2
    c = cos[None, :, None, :]
    s = sin[None, :, None, :]
    x1 = x[..., :half]
    x2 = x[..., half:ROT_DIM]
    return jnp.concatenate([x1 * c - x2 * s, x2 * c + x1 * s, x[..., ROT_DIM:]], axis=-1)


def banded_attention(q, k, v, half_window, sink=None):
    bsz, n, hkv, grp, hd = q.shape
    blk = half_window
    nb = -(-n // blk)
    npad = nb * blk
    q = jnp.pad(q, ((0, 0), (0, npad - n), (0, 0), (0, 0), (0, 0)))
    kv_pad = ((0, 0), (blk, npad - n + blk), (0, 0), (0, 0))
    kb = jnp.pad(k, kv_pad).reshape(bsz, nb + 2, blk, hkv, hd)
    vb = jnp.pad(v, kv_pad).reshape(bsz, nb + 2, blk, hkv, hd)
    kw = jnp.concatenate([kb[:, :-2], kb[:, 1:-1], kb[:, 2:]], axis=2)
    vw = jnp.concatenate([vb[:, :-2], vb[:, 1:-1], vb[:, 2:]], axis=2)
    qb = q.reshape(bsz, nb, blk, hkv, grp, hd)
    s = jnp.einsum('bnqhgd,bnkhd->bnhgqk', qb, kw) * (hd ** -0.5)
    qpos = jnp.arange(nb)[:, None, None] * blk + jnp.arange(blk)[None, :, None]
    kpos = jnp.arange(nb)[:, None, None] * blk - blk + jnp.arange(3 * blk)[None, None, :]
    valid = (jnp.abs(kpos - qpos) <= half_window) & (kpos >= 0) & (kpos < n)
    s = jnp.where(valid[None, :, None, None], s, -jnp.inf)
    m = s.max(axis=-1)
    if sink is not None:
        sk = sink.reshape(hkv, grp)[None, None, :, :, None]
        m = jnp.maximum(m, sk)
    p = jnp.exp(s - m[..., None])
    den = p.sum(axis=-1)
    if sink is not None:
        den = den + jnp.exp(sk - m)
    o = jnp.einsum('bnhgqk,bnkhd->bnqhgd', p, vw) / jnp.moveaxis(den, -1, 2)[..., None]
    lse = jnp.moveaxis(m + jnp.log(den), -1, 2)
    o = o.reshape(bsz, npad, hkv, grp, hd)[:, :n]
    lse = lse.reshape(bsz, npad, hkv, grp)[:, :n]
    return o, lse


def depthwise_conv(x, w):
    kw = w.shape[0]
    return lax.conv_general_dilated(x, w.astype(x.dtype)[:, None, :], window_strides=(1,),
                                    padding=((kw // 2, kw // 2),),
                                    dimension_numbers=('NWC', 'WIO', 'NWC'),
                                    feature_group_count=x.shape[-1])


def mlstm_chunkwise(q, k, v, logi, logf):
    g_, h_, n, d = q.shape
    L = B_CHUNK
    nc = n // L

    def chunks(t):
        return jnp.moveaxis(t.reshape(t.shape[:2] + (nc, L) + t.shape[3:]), 2, 0)

    b = jnp.cumsum(chunks(logf), axis=-1)
    tril = jnp.tril(jnp.ones((L, L), dtype=bool))

    def step(carry, inp):
        c_st, n_st, m_st = carry
        qc, kc, vc, ic, bc = inp
        logd = jnp.where(tril, bc[..., :, None] - bc[..., None, :] + ic[..., None, :], -jnp.inf)
        m_inter = bc + m_st[..., None]
        mt = jnp.maximum(m_inter, logd.max(axis=-1))
        s = jnp.einsum('ghtd,ghsd->ghts', qc, kc) * jnp.exp(logd - mt[..., None])
        sc = jnp.exp(m_inter - mt)
        num = jnp.einsum('ghts,ghse->ghte', s, vc) + sc[..., None] * jnp.einsum('ghtd,ghde->ghte', qc, c_st)
        den = s.sum(axis=-1) + sc * jnp.einsum('ghtd,ghd->ght', qc, n_st)
        h = num / jnp.maximum(jnp.abs(den), jnp.exp(-mt))[..., None]
        bl = bc[..., -1]
        logw = bl[..., None] - bc + ic
        m_new = jnp.maximum(bl + m_st, logw.max(axis=-1))
        w = jnp.exp(logw - m_new[..., None])
        dec = jnp.exp(bl + m_st - m_new)
        kw = kc * w[..., None]
        c_st = dec[..., None, None] * c_st + jnp.einsum('ghsd,ghse->ghde', kw, vc)
        n_st = dec[..., None] * n_st + kw.sum(axis=2)
        return (c_st, n_st, m_new), h

    init = (jnp.zeros((g_, h_, d, d), jnp.float32), jnp.zeros((g_, h_, d), jnp.float32),
            jnp.zeros((g_, h_), jnp.float32))
    _, hs = lax.scan(step, init, (chunks(q), chunks(k), chunks(v), chunks(logi), b))
    return jnp.moveaxis(hs, 0, 2).reshape(g_, h_, n, d)


def mlstm_bidirectional(q, k, v, g):
    bsz, n, _ = q.shape

    def heads(t):
        return t.astype(jnp.float32).reshape(bsz, n, B_HEADS, B_HEAD_DIM).transpose(0, 2, 1, 3)

    q = heads(q) * (B_HEAD_DIM ** -0.5)
    k = heads(k)
    v = heads(v)
    g = g.reshape(bsz, n, 4, B_HEADS).transpose(2, 0, 3, 1)
    flip = lambda t: jnp.flip(t, axis=2)
    both = lambda fwd, bwd: jnp.concatenate([fwd, flip(bwd)], axis=0)
    h2 = mlstm_chunkwise(both(q, q), both(k, k), both(v, v), both(g[0], g[2]),
                         jax.nn.log_sigmoid(both(g[1], g[3])))
    return h2[:bsz] + flip(h2[bsz:])


def head_layernorm(h, gain):
    bsz, nh, n, d = h.shape
    mu = h.mean(axis=-1, keepdims=True)
    hc = h - mu
    y = hc * lax.rsqrt(jnp.mean(hc * hc, axis=-1, keepdims=True) + EPS)
    y = y * gain.astype(jnp.float32).reshape(nh, 1, d)
    return y.transpose(0, 2, 1, 3).reshape(bsz, n, nh * d)


def mixer_even(u, w_in, b_gate, conv_w, sink, head_gain, w_out, cos, sin):
    bsz, n, _ = u.shape
    f32 = jnp.float32
    z = u @ w_in
    qa, ka, va, qkb, vb, ob, gates = jnp.split(z, EVEN_SPLITS, axis=-1)
    qa = partial_rope(qa.astype(f32).reshape(bsz, n, A_Q_HEADS, HEAD_DIM), cos, sin)
    ka = partial_rope(ka.astype(f32).reshape(bsz, n, A_KV_HEADS, HEAD_DIM), cos, sin)
    va = va.astype(f32).reshape(bsz, n, A_KV_HEADS, HEAD_DIM)
    qa = qa.reshape(bsz, n, A_KV_HEADS, A_Q_HEADS // A_KV_HEADS, HEAD_DIM)
    ya, _ = banded_attention(qa, ka, va, A_HALF_WINDOW, sink.astype(f32))
    ya = ya.reshape(bsz, n, A_Q_W)
    qkb = jax.nn.silu(depthwise_conv(qkb, conv_w))
    qb, kb = jnp.split(qkb, 2, axis=-1)
    hb = mlstm_bidirectional(qb, kb, vb, gates.astype(f32) + b_gate.astype(f32))
    yb = jax.nn.sigmoid(ob.astype(f32)) * head_layernorm(hb, head_gain)
    y = jnp.concatenate([ya, yb], axis=-1).astype(u.dtype)
    return y @ w_out


def to_residues(t, dil):
    b, s = t.shape[:2]
    rest = t.shape[2:]
    t = jnp.swapaxes(t.reshape((b, s // dil, dil) + rest), 1, 2)
    return t.reshape((b * dil, s // dil) + rest)


def from_residues(t, b, dil):
    bd, sd = t.shape[:2]
    rest = t.shape[2:]
    t = jnp.swapaxes(t.reshape((b, dil, sd) + rest), 1, 2)
    return t.reshape((b, sd * dil) + rest)


def mixer_odd(u, w_in, w_out, cos, sin):
    bsz, n, _ = u.shape
    f32 = jnp.float32
    q, k, v = jnp.split((u @ w_in).astype(f32), 3, axis=-1)
    q = partial_rope(q.reshape(bsz, n, C_HEADS, HEAD_DIM), cos, sin)[:, :, :, None, :]
    k = partial_rope(k.reshape(bsz, n, C_HEADS, HEAD_DIM), cos, sin)
    v = v.reshape(bsz, n, C_HEADS, HEAD_DIM)
    outs, lses = [], []
    for window, dil in C_PATTERNS:
        half = window // (2 * dil)
        o, lse = banded_attention(to_residues(q, dil), to_residues(k, dil), to_residues(v, dil), half)
        outs.append(from_residues(o, bsz, dil))
        lses.append(from_residues(lse, bsz, dil))
    wts = jax.nn.softmax(jnp.stack(lses), axis=0)
    y = jnp.sum(wts[..., None] * jnp.stack(outs), axis=0).reshape(bsz, n, C_W)
    return y.astype(u.dtype) @ w_out


def cross_attention(u, memn, wq, wkv, wo):
    bsz, n, _ = u.shape
    m = memn.shape[1]
    q = (u @ wq).astype(jnp.float32).reshape(bsz, n, X_HEADS, HEAD_DIM)
    k, v = jnp.split((memn @ wkv).astype(jnp.float32), 2, axis=-1)
    k = k.reshape(bsz, m, X_HEADS, HEAD_DIM)
    v = v.reshape(bsz, m, X_HEADS, HEAD_DIM)
    p = jax.nn.softmax(jnp.einsum('bqhd,bkhd->bhqk', q, k) * (HEAD_DIM ** -0.5), axis=-1)
    o = jnp.einsum('bhqk,bkhd->bqhd', p, v).reshape(bsz, n, X_W)
    return o.astype(u.dtype) @ wo


def swiglu(t, w_gu, w_down):
    gate, up = jnp.split(t @ w_gu, 2, axis=-1)
    return (jax.nn.silu(gate) * up) @ w_down


def moe_swiglu(u, w_router, w_gu, w_down):
    bsz, n, dm = u.shape
    t = u.reshape(bsz * n, dm)
    logits = (t @ w_router).astype(jnp.float32)
    top_v, top_i = lax.top_k(logits, TOP_K)
    gates = jax.nn.softmax(top_v, axis=-1)
    combine = jnp.sum(jax.nn.one_hot(top_i, N_EXPERTS, dtype=jnp.float32) * gates[..., None], axis=1)
    y = jnp.zeros((bsz * n, dm), jnp.float32)
    for e in range(N_EXPERTS):
        y = y + combine[:, e:e + 1] * swiglu(t, w_gu[e], w_down[e]).astype(jnp.float32)
    return y.reshape(bsz, n, dm).astype(u.dtype)


def setup_inputs(seed: int = 0) -> dict:
    key = jax.random.key(seed)
    ks = iter(list(jax.random.split(key, 40)))
    f32 = jnp.float32
    ne = (DEPTH + 1) // 2
    no = DEPTH // 2

    def dense(shape, fan_in):
        return jax.random.normal(next(ks), shape, f32) * (fan_in ** -0.5)

    def gain(shape):
        return 1.0 + 0.02 * jax.random.normal(next(ks), shape, f32)

    x = jax.random.normal(next(ks), (BATCH, SEQ, D_MODEL), f32)
    mem = jax.random.normal(next(ks), (BATCH, MEM_LEN, D_MODEL), f32)
    ib = 0.1 * jax.random.normal(next(ks), (ne, 2, B_HEADS), f32)
    fb = 3.0 + 3.0 * jax.random.uniform(next(ks), (ne, 2, B_HEADS), f32)
    b_gate = jnp.stack([ib[:, 0], fb[:, 0], ib[:, 1], fb[:, 1]], axis=1).reshape(ne, 4 * B_HEADS)
    return {
        "x": x,
        "mem": mem,
        "ln_mix": gain((DEPTH, D_MODEL)),
        "ln_xattn": gain((DEPTH, D_MODEL)),
        "ln_mem": gain((DEPTH, D_MODEL)),
        "ln_ffn": gain((DEPTH, D_MODEL)),
        "ln_final": gain((D_MODEL,)),
        "ev_w_in": dense((ne, D_MODEL, EVEN_IN), D_MODEL),
        "ev_b_gate": b_gate,
        "ev_conv": dense((ne, B_CONV, 2 * B_W), B_CONV),
        "ev_sink": 0.5 * jax.random.normal(next(ks), (ne, A_Q_HEADS), f32),
        "ev_head_norm": gain((ne, B_W)),
        "ev_w_out": dense((ne, EVEN_OUT, D_MODEL), EVEN_OUT),
        "ffn_w_gu": dense((ne, D_MODEL, 2 * FFN_DIM), D_MODEL),
        "ffn_w_down": dense((ne, FFN_DIM, D_MODEL), FFN_DIM),
        "od_w_in": dense((no, D_MODEL, 3 * C_W), D_MODEL),
        "od_w_out": dense((no, C_W, D_MODEL), C_W),
        "moe_router": dense((no, D_MODEL, N_EXPERTS), D_MODEL),
        "moe_w_gu": dense((no, N_EXPERTS, D_MODEL, 2 * FFN_DIM), D_MODEL),
        "moe_w_down": dense((no, N_EXPERTS, FFN_DIM, D_MODEL), FFN_DIM),
        "x_wq": dense((DEPTH, D_MODEL, X_W), D_MODEL),
        "x_wkv": dense((DEPTH, D_MODEL, 2 * X_W), D_MODEL),
        "x_wo": dense((DEPTH, X_W, D_MODEL), X_W),
    }


def reference(x, mem, ln_mix, ln_xattn, ln_mem, ln_ffn, ln_final, ev_w_in, ev_b_gate, ev_conv, ev_sink,
              ev_head_norm, ev_w_out, ffn_w_gu, ffn_w_down, od_w_in, od_w_out, moe_router, moe_w_gu,
              moe_w_down, x_wq, x_wkv, x_wo):
    cos, sin = rope_tables(x.shape[1])
    for layer in range(DEPTH):
        j = layer // 2
        u = rmsnorm(x, ln_mix[layer])
        if layer % 2 == 0:
            y = mixer_even(u, ev_w_in[j], ev_b_gate[j], ev_conv[j], ev_sink[j], ev_head_norm[j], ev_w_out[j], cos, sin)
        else:
            y = mixer_odd(u, od_w_in[j], od_w_out[j], cos, sin)
        x = x + y.astype(x.dtype)
        u = rmsnorm(x, ln_xattn[layer])
        x = x + cross_attention(u, rmsnorm(mem, ln_mem[layer]), x_wq[layer], x_wkv[layer], x_wo[layer]).astype(x.dtype)
        u = rmsnorm(x, ln_ffn[layer])
        if layer % 2 == 0:
            y = swiglu(u, ffn_w_gu[j], ffn_w_down[j])
        else:
            y = moe_swiglu(u, moe_router[j], moe_w_gu[j], moe_w_down[j])
        x = x + y.astype(x.dtype)
    return rmsnorm(x, ln_final)
```

```python
import functools
import math

import jax
import jax.numpy as jnp
from jax import lax
from jax.experimental import pallas as pl
from jax.experimental.pallas import tpu as pltpu

D_MODEL = 2048
HEAD_DIM = 128
A_Q_HEADS = 8
A_KV_HEADS = 2
A_HALF_WINDOW = 128
B_HEADS = 4
B_HEAD_DIM = 256
B_CONV = 5
C_HEADS = D_MODEL // HEAD_DIM
C_PATTERNS = ((128, 1), (512, 4), (2048, 16))
X_HEADS = 4
FFN_DIM = 7168
N_EXPERTS = 8
ROPE_THETA = 500000.0
ROT_DIM = HEAD_DIM // 4
EPS = 1e-6

A_Q_W = A_Q_HEADS * HEAD_DIM
A_KV_W = A_KV_HEADS * HEAD_DIM
B_W = B_HEADS * B_HEAD_DIM
X_W = X_HEADS * HEAD_DIM
EVEN_MAIN = A_Q_W + 2 * A_KV_W + 4 * B_W
COL_KA = A_Q_W
COL_VA = A_Q_W + A_KV_W
COL_QKB = A_Q_W + 2 * A_KV_W
COL_VB = COL_QKB + 2 * B_W
COL_OB = COL_VB + B_W

LANES = 128
V7X_VMEM_BYTES = 64 * 1024 * 1024
VMEM_CAP = V7X_VMEM_BYTES - 8 * 1024 * 1024

NEG = -1e30
BF16 = jnp.bfloat16
F32 = jnp.float32


def _params(sem, vmem_bytes):
    return pltpu.CompilerParams(dimension_semantics=sem, vmem_limit_bytes=int(min(VMEM_CAP, vmem_bytes)))


def _nbytes(shape, dtype):
    return math.prod(shape) * jnp.dtype(dtype).itemsize


def _dot(a, b):
    return jnp.dot(a, b, preferred_element_type=F32)


def _dot_nt(a, b):
    return lax.dot_general(a, b, (((1,), (1,)), ((), ())), preferred_element_type=F32)


def _dot_tn(a, b):
    return lax.dot_general(a, b, (((0,), (0,)), ((), ())), preferred_element_type=F32)


def _rms(x, g):
    return x * lax.rsqrt(jnp.mean(x * x, axis=-1, keepdims=True) + EPS) * g


def _rope_tile(z, c, s1, s2):
    return z * c + pltpu.roll(z, LANES - ROT_DIM // 2, 1) * s1 + pltpu.roll(z, ROT_DIM // 2, 1) * s2


def _norm_mm_kernel(x_ref, g_ref, w_ref, c_ref, s1_ref, s2_ref, o_ref, xn_ref, *, tn, n_rope, n_scale, scale):
    j = pl.program_id(1)

    @pl.when(j == 0)
    def _():
        xn_ref[...] = _rms(x_ref[...], g_ref[...]).astype(BF16)

    z = _dot(xn_ref[...], w_ref[...].astype(BF16))
    heads = tn // LANES
    if n_rope == 0:
        o_ref[...] = z.astype(o_ref.dtype)
        return
    for hh in range(heads):
        zt = z[:, hh * LANES:(hh + 1) * LANES]
        gh = j * heads + hh
        sl = slice(hh * LANES, (hh + 1) * LANES)

        @pl.when(gh < n_scale)
        def _():
            o_ref[:, sl] = (_rope_tile(zt, c_ref[...], s1_ref[...], s2_ref[...]) * scale).astype(o_ref.dtype)

        @pl.when((gh >= n_scale) & (gh < n_rope))
        def _():
            o_ref[:, sl] = _rope_tile(zt, c_ref[...], s1_ref[...], s2_ref[...]).astype(o_ref.dtype)

        @pl.when(gh >= n_rope)
        def _():
            o_ref[:, sl] = zt.astype(o_ref.dtype)


def norm_matmul(x, g, w3, widx, n_out, *, rope=None, n_rope=0, n_scale=0, scale=1.0, seq=None, tm=1024, tn=512,
                out_dtype=BF16, name="norm_mm"):
    m, k = x.shape
    tm = min(tm, m)
    tn = min(tn, n_out)
    assert m % tm == 0 and n_out % tn == 0 and tn % LANES == 0
    if rope is None:
        dummy = jnp.zeros((8, LANES), F32)
        rope = (dummy, dummy, dummy)
        rspec = pl.BlockSpec((8, LANES), lambda i, j: (0, 0))
    else:
        nsb = seq // tm
        assert seq % tm == 0
        rspec = pl.BlockSpec((tm, LANES), lambda i, j: (i % nsb, 0))
    vm = 2 * _nbytes((tm, k), F32) + _nbytes((tm, k), BF16) + 2 * _nbytes((k, tn), F32) + _nbytes((k, tn), BF16) \
        + 2 * _nbytes((tm, tn), out_dtype) + 2 * _nbytes((tm, tn), F32) + 6 * _nbytes((tm, LANES), F32) + (4 << 20)
    return pl.pallas_call(
        functools.partial(_norm_mm_kernel, tn=tn, n_rope=n_rope, n_scale=n_scale, scale=scale),
        grid=(m // tm, n_out // tn),
        in_specs=[pl.BlockSpec((tm, k), lambda i, j: (i, 0)),
                  pl.BlockSpec((1, k), lambda i, j: (0, 0)),
                  pl.BlockSpec((None, k, tn), lambda i, j: (widx, 0, j)),
                  rspec, rspec, rspec],
        out_specs=pl.BlockSpec((tm, tn), lambda i, j: (i, j)),
        out_shape=jax.ShapeDtypeStruct((m, n_out), out_dtype),
        scratch_shapes=[pltpu.VMEM((tm, k), BF16)],
        compiler_params=_params(("arbitrary", "arbitrary"), vm),
        name=name,
    )(x, g.reshape(1, k), w3, *rope)


def _mm_resid_kernel(*refs, n_x):
    xs = refs[:n_x]
    ws = refs[n_x:2 * n_x]
    r_ref = refs[2 * n_x]
    o_ref = refs[2 * n_x + 1]
    acc = r_ref[...]
    for x_ref, w_ref in zip(xs, ws):
        acc = acc + _dot(x_ref[...].astype(BF16), w_ref[...].astype(BF16))
    o_ref[...] = acc


def matmul_resid(xs, w3, widx, resid, *, tm=512, tn=512, name="mm_resid"):
    m, kx = xs[0].shape
    n = w3.shape[2]
    n_x = len(xs)
    assert w3.shape[1] == n_x * kx and m % tm == 0 and n % tn == 0
    in_specs = [pl.BlockSpec((tm, kx), lambda i, j: (i, 0)) for _ in xs]
    in_specs += [pl.BlockSpec((None, kx, tn), lambda i, j, q=q: (widx, q, j)) for q in range(n_x)]
    in_specs += [pl.BlockSpec((tm, tn), lambda i, j: (i, j))]
    vm = n_x * (2 * _nbytes((tm, kx), xs[0].dtype) + 3 * _nbytes((kx, tn), F32)) + 6 * _nbytes((tm, tn), F32) + (4 << 20)
    return pl.pallas_call(
        functools.partial(_mm_resid_kernel, n_x=n_x),
        grid=(m // tm, n // tn),
        in_specs=in_specs,
        out_specs=pl.BlockSpec((tm, tn), lambda i, j: (i, j)),
        out_shape=jax.ShapeDtypeStruct((m, n), F32),
        compiler_params=_params(("arbitrary", "arbitrary"), vm),
        name=name,
    )(*xs, *([w3] * n_x), resid)


def _rmsnorm_kernel(x_ref, g_ref, o_ref):
    o_ref[...] = _rms(x_ref[...], g_ref[...]).astype(o_ref.dtype)


def rmsnorm(x, g, out_dtype, *, tm=512, name="rmsnorm"):
    m, k = x.shape
    tm = min(tm, m)
    assert m % tm == 0
    vm = 4 * _nbytes((tm, k), F32) + 2 * _nbytes((tm, k), out_dtype) + (4 << 20)
    return pl.pallas_call(
        _rmsnorm_kernel,
        grid=(m // tm,),
        in_specs=[pl.BlockSpec((tm, k), lambda i: (i, 0)), pl.BlockSpec((1, k), lambda i: (0, 0))],
        out_specs=pl.BlockSpec((tm, k), lambda i: (i, 0)),
        out_shape=jax.ShapeDtypeStruct((m, k), out_dtype),
        compiler_params=_params(("arbitrary",), vm),
        name=name,
    )(x, g.reshape(1, k))


def _win_attn_kernel(sink_ref, q_ref, kp_ref, kc_ref, kn_ref, vp_ref, vc_ref, vn_ref, o_ref, *, tq, seq, grp):
    i = pl.program_id(1)
    kv = pl.program_id(2)
    hw = A_HALF_WINDOW
    k = jnp.concatenate([kp_ref[...], kc_ref[...], kn_ref[...]], axis=0)
    v = jnp.concatenate([vp_ref[...], vc_ref[...], vn_ref[...]], axis=0)
    wk = tq + 2 * hw
    qpos = i * tq + lax.broadcasted_iota(jnp.int32, (tq, wk), 0)
    kpos = i * tq - hw + lax.broadcasted_iota(jnp.int32, (tq, wk), 1)
    valid = (jnp.abs(kpos - qpos) <= hw) & (kpos >= 0) & (kpos < seq)
    for g in range(grp):
        sk = sink_ref[kv * grp + g]
        s = _dot_nt(q_ref[:, g * HEAD_DIM:(g + 1) * HEAD_DIM], k)
        s = jnp.where(valid, s, NEG)
        m = jnp.maximum(jnp.max(s, axis=1, keepdims=True), sk)
        p = jnp.exp(s - m)
        den = jnp.sum(p, axis=1, keepdims=True) + jnp.exp(sk - m)
        o = _dot(p.astype(BF16), v) / den
        o_ref[:, g * HEAD_DIM:(g + 1) * HEAD_DIM] = o.astype(o_ref.dtype)


def window_attention(z, sink, bsz, seq, *, tq=256, name="win_attn"):
    t = z.shape[0]
    hw = A_HALF_WINDOW
    grp = A_Q_HEADS // A_KV_HEADS
    nq = seq // tq
    r = tq // hw
    nhb = seq // hw
    qw = grp * HEAD_DIM
    kcol = COL_KA // HEAD_DIM
    vcol = COL_VA // HEAD_DIM
    cur = lambda col: pl.BlockSpec((tq, HEAD_DIM), lambda b, i, kv, s: (b * nq + i, col + kv))
    prev = lambda col: pl.BlockSpec((hw, HEAD_DIM), lambda b, i, kv, s: (b * nhb + jnp.maximum(i * r - 1, 0), col + kv))
    nxt = lambda col: pl.BlockSpec((hw, HEAD_DIM), lambda b, i, kv, s: (b * nhb + jnp.minimum((i + 1) * r, nhb - 1), col + kv))
    grid_spec = pltpu.PrefetchScalarGridSpec(
        num_scalar_prefetch=1,
        grid=(bsz, nq, A_KV_HEADS),
        in_specs=[pl.BlockSpec((tq, qw), lambda b, i, kv, s: (b * nq + i, kv)),
                  prev(kcol), cur(kcol), nxt(kcol), prev(vcol), cur(vcol), nxt(vcol)],
        out_specs=pl.BlockSpec((tq, qw), lambda b, i, kv, s: (b * nq + i, kv)),
    )
    return pl.pallas_call(
        functools.partial(_win_attn_kernel, tq=tq, seq=seq, grp=grp),
        grid_spec=grid_spec,
        out_shape=jax.ShapeDtypeStruct((t, A_Q_W), BF16),
        compiler_params=_params(("arbitrary",) * 3, 32 << 20),
        name=name,
    )(sink.astype(F32), z, z, z, z, z, z, z)


def _dilated_bias(tq, wk, reach):
    row = lax.broadcasted_iota(jnp.int32, (tq, wk), 0)
    col = lax.broadcasted_iota(jnp.int32, (tq, wk), 1)
    d = col - reach - row
    ad = jnp.abs(d)
    mult = jnp.zeros((tq, wk), jnp.int32)
    for window, dil in C_PATTERNS:
        mult = mult + ((ad <= window // 2) & ((d & (dil - 1)) == 0)).astype(jnp.int32)
    bias = jnp.where(mult == 1, 0.0, jnp.where(mult == 2, math.log(2.0), math.log(3.0)))
    return jnp.where(mult == 0, NEG, bias).astype(F32)


def _dil_attn_kernel(q_ref, k_ref, v_ref, o_ref, kp_ref, vp_ref, bias_ref, *, tq, seq, reach):
    b = pl.program_id(0)
    h = pl.program_id(1)
    i = pl.program_id(2)
    wk = tq + 2 * reach

    @pl.when((b == 0) & (h == 0) & (i == 0))
    def _():
        bias_ref[...] = _dilated_bias(tq, wk, reach)
        zeros = jnp.zeros((reach, HEAD_DIM), BF16)
        kp_ref[0:reach, :] = zeros
        kp_ref[reach + seq:, :] = zeros
        vp_ref[0:reach, :] = zeros
        vp_ref[reach + seq:, :] = zeros

    @pl.when(i == 0)
    def _():
        kp_ref[reach:reach + seq, :] = k_ref[...]
        vp_ref[reach:reach + seq, :] = v_ref[...]

    start = pl.multiple_of(i * tq, tq)
    ks = kp_ref[pl.ds(start, wk), :]
    vs = vp_ref[pl.ds(start, wk), :]
    kpos = i * tq - reach + lax.broadcasted_iota(jnp.int32, (1, wk), 1)
    colbias = jnp.where((kpos >= 0) & (kpos < seq), 0.0, NEG).astype(F32)
    s = _dot_nt(q_ref[...], ks) + bias_ref[...] + colbias
    m = jnp.max(s, axis=1, keepdims=True)
    p = jnp.exp(s - m)
    den = jnp.sum(p, axis=1, keepdims=True)
    o_ref[...] = (_dot(p.astype(BF16), vs) / den).astype(o_ref.dtype)


def dilated_attention(z, bsz, seq, *, tq=256, name="dil_attn"):
    t = z.shape[0]
    reach = max(w // 2 for w, _ in C_PATTERNS)
    nq = seq // tq
    wk = tq + 2 * reach
    assert seq % tq == 0 and seq >= wk
    vm = 4 * _nbytes((seq, HEAD_DIM), BF16) + 2 * _nbytes((seq + 2 * reach, HEAD_DIM), BF16) \
        + 6 * _nbytes((tq, wk), F32) + (8 << 20)
    return pl.pallas_call(
        functools.partial(_dil_attn_kernel, tq=tq, seq=seq, reach=reach),
        grid=(bsz, C_HEADS, nq),
        in_specs=[pl.BlockSpec((tq, HEAD_DIM), lambda b, h, i: (b * nq + i, h)),
                  pl.BlockSpec((seq, HEAD_DIM), lambda b, h, i: (b, C_HEADS + h)),
                  pl.BlockSpec((seq, HEAD_DIM), lambda b, h, i: (b, 2 * C_HEADS + h))],
        out_specs=pl.BlockSpec((tq, HEAD_DIM), lambda b, h, i: (b * nq + i, h)),
        out_shape=jax.ShapeDtypeStruct((t, D_MODEL), BF16),
        scratch_shapes=[pltpu.VMEM((seq + 2 * reach, HEAD_DIM), BF16),
                        pltpu.VMEM((seq + 2 * reach, HEAD_DIM), BF16),
                        pltpu.VMEM((tq, wk), F32)],
        compiler_params=_params(("arbitrary",) * 3, vm),
        name=name,
    )(z, z, z)


def _xattn_kernel(x_ref, g_ref, wq_ref, kv_ref, wo_ref, o_ref, wq_s, wo_s):
    @pl.when(pl.program_id(0) == 0)
    def _():
        wq_s[...] = wq_ref[...].astype(BF16)
        wo_s[...] = wo_ref[...].astype(BF16)

    x = x_ref[...]
    xn = _rms(x, g_ref[...]).astype(BF16)
    q = (_dot(xn, wq_s[...]) * (HEAD_DIM ** -0.5)).astype(BF16)
    kv = kv_ref[...]
    outs = []
    for h in range(X_HEADS):
        k = kv[:, h * HEAD_DIM:(h + 1) * HEAD_DIM]
        v = kv[:, X_W + h * HEAD_DIM:X_W + (h + 1) * HEAD_DIM]
        s = _dot_nt(q[:, h * HEAD_DIM:(h + 1) * HEAD_DIM], k)
        m = jnp.max(s, axis=1, keepdims=True)
        p = jnp.exp(s - m)
        den = jnp.sum(p, axis=1, keepdims=True)
        outs.append((_dot(p.astype(BF16), v) / den).astype(BF16))
    o = jnp.concatenate(outs, axis=1)
    o_ref[...] = x + _dot(o, wo_s[...])


def cross_attention(x, g, wq3, wo3, layer, kv, seq, mem_len, *, tm=512, name="xattn"):
    t, d = x.shape
    nsb = seq // tm
    assert seq % tm == 0
    vm = 4 * _nbytes((tm, d), F32) + 3 * _nbytes((d, X_W), F32) * 2 + 6 * _nbytes((tm, d), F32) + (8 << 20)
    return pl.pallas_call(
        _xattn_kernel,
        grid=(t // tm,),
        in_specs=[pl.BlockSpec((tm, d), lambda i: (i, 0)),
                  pl.BlockSpec((1, d), lambda i: (0, 0)),
                  pl.BlockSpec((None, d, X_W), lambda i: (layer, 0, 0)),
                  pl.BlockSpec((mem_len, 2 * X_W), lambda i: (i // nsb, 0)),
                  pl.BlockSpec((None, X_W, d), lambda i: (layer, 0, 0))],
        out_specs=pl.BlockSpec((tm, d), lambda i: (i, 0)),
        out_shape=jax.ShapeDtypeStruct((t, d), F32),
        scratch_shapes=[pltpu.VMEM((d, X_W), BF16), pltpu.VMEM((X_W, d), BF16)],
        compiler_params=_params(("arbitrary",), vm),
        name=name,
    )(x, g.reshape(1, d), wq3, kv, wo3)


def _conv_kernel(xp_ref, xc_ref, xn_ref, w_ref, o_ref, *, ts, ns, halo, q_scale, n_q_blocks):
    i = pl.program_id(1)
    j = pl.program_id(2)
    pad = B_CONV // 2
    xp = jnp.where(i > 0, xp_ref[halo - pad:, :].astype(F32), 0.0)
    xn = jnp.where(i < ns - 1, xn_ref[:pad, :].astype(F32), 0.0)
    xx = jnp.concatenate([xp, xc_ref[...].astype(F32), xn], axis=0)
    w = w_ref[...]
    acc = xx[0:ts, :] * w[0:1, :]
    for tap in range(1, B_CONV):
        acc = acc + xx[tap:tap + ts, :] * w[tap:tap + 1, :]
    y = acc * jax.nn.sigmoid(acc)
    y = y * jnp.where(j < n_q_blocks, q_scale, 1.0)
    o_ref[...] = y.astype(o_ref.dtype)


def conv_silu(z, conv_w8, bsz, seq, *, ts=512, tc=512, name="conv_silu"):
    t = z.shape[0]
    halo = 16
    ns = seq // ts
    nh = seq // halo
    cb = COL_QKB // tc
    assert COL_QKB % tc == 0 and seq % ts == 0 and B_W % tc == 0
    return pl.pallas_call(
        functools.partial(_conv_kernel, ts=ts, ns=ns, halo=halo, q_scale=B_HEAD_DIM ** -0.5, n_q_blocks=B_W // tc),
        grid=(bsz, ns, 2 * B_W // tc),
        in_specs=[pl.BlockSpec((halo, tc), lambda b, i, j: (b * nh + jnp.maximum(i * (ts // halo) - 1, 0), cb + j)),
                  pl.BlockSpec((ts, tc), lambda b, i, j: (b * ns + i, cb + j)),
                  pl.BlockSpec((halo, tc), lambda b, i, j: (b * nh + jnp.minimum((i + 1) * (ts // halo), nh - 1), cb + j)),
                  pl.BlockSpec((8, tc), lambda b, i, j: (0, j))],
        out_specs=pl.BlockSpec((ts, tc), lambda b, i, j: (b * ns + i, j)),
        out_shape=jax.ShapeDtypeStruct((t, 2 * B_W), BF16),
        compiler_params=_params(("arbitrary",) * 3, 32 << 20),
        name=name,
    )(z, z, z, conv_w8)


def _log_sigmoid(x):
    return jnp.minimum(x, 0.0) - jnp.log(1.0 + jnp.exp(-jnp.abs(x)))


def _mlstm_chunk(q, k, v, i_col, f_col, i_row, f_row, c_ref, n_ref, m_ref, rev):
    L = q.shape[0]
    logf_c = _log_sigmoid(f_col)
    logf_r = _log_sigmoid(f_row)
    row = lax.broadcasted_iota(jnp.int32, (L, L), 0)
    col = lax.broadcasted_iota(jnp.int32, (L, L), 1)
    causal = (col >= row) if rev else (col <= row)
    causal_t = (row >= col) if rev else (row <= col)
    b_col = jnp.sum(jnp.where(causal, logf_r, 0.0), axis=1, keepdims=True)
    b_row = jnp.sum(jnp.where(causal_t, logf_c, 0.0), axis=0, keepdims=True)
    b_all = jnp.sum(logf_r, axis=1, keepdims=True)
    m_st = m_ref[...]
    c_st = c_ref[...]
    n_st = n_ref[...]
    logd = jnp.where(causal, b_col - b_row + i_row, NEG)
    m_inter = b_col + m_st
    mt = jnp.maximum(m_inter, jnp.max(logd, axis=1, keepdims=True))
    s = _dot_nt(q, k) * jnp.exp(logd - mt)
    sc = jnp.exp(m_inter - mt)
    num = _dot(s.astype(BF16), v) + sc * _dot(q, c_st.astype(BF16))
    den = jnp.sum(s, axis=1, keepdims=True) + sc * jnp.sum(q.astype(F32) * n_st, axis=1, keepdims=True)
    h = num / jnp.maximum(jnp.abs(den), jnp.exp(-mt))
    logw = b_all - b_col + i_col
    m_new = jnp.maximum(b_all + m_st, jnp.max(logw, axis=0, keepdims=True))
    wgt = jnp.exp(logw - m_new)
    dec = jnp.exp(b_all + m_st - m_new)
    kw = k.astype(F32) * wgt
    c_ref[...] = dec * c_st + _dot_tn(kw.astype(BF16), v)
    n_ref[...] = dec * n_st + jnp.sum(kw, axis=0, keepdims=True)
    m_ref[...] = m_new
    return h


def _mlstm_kernel(bias_ref, qf_ref, kf_ref, vf0_ref, vf1_ref, gcf_ref, grf_ref,
                  qb_ref, kb_ref, vb0_ref, vb1_ref, gcb_ref, grb_ref,
                  hf_ref, hb_ref, c_ref, n_ref, m_ref):
    @pl.when(pl.program_id(1) == 0)
    def _():
        c_ref[...] = jnp.zeros_like(c_ref)
        n_ref[...] = jnp.zeros_like(n_ref)
        m_ref[...] = jnp.zeros_like(m_ref)

    lane = lax.broadcasted_iota(jnp.int32, (1, LANES), 1)
    dh = B_HEAD_DIM
    for d, (q_ref, k_ref, v0_ref, v1_ref, gc_ref, gr_ref, h_ref) in enumerate((
            (qf_ref, kf_ref, vf0_ref, vf1_ref, gcf_ref, grf_ref, hf_ref),
            (qb_ref, kb_ref, vb0_ref, vb1_ref, gcb_ref, grb_ref, hb_ref))):
        gc = gc_ref[...]
        for hd in range(B_HEADS):
            ci = 2 * d * B_HEADS + hd
            cf = ci + B_HEADS
            i_col = jnp.sum(jnp.where(lane == ci, gc, 0.0), axis=1, keepdims=True) + bias_ref[ci]
            f_col = jnp.sum(jnp.where(lane == cf, gc, 0.0), axis=1, keepdims=True) + bias_ref[cf]
            i_row = gr_ref[ci:ci + 1, :] + bias_ref[ci]
            f_row = gr_ref[cf:cf + 1, :] + bias_ref[cf]
            v_ref = v0_ref if hd < B_HEADS // 2 else v1_ref
            vo = (hd % (B_HEADS // 2)) * dh
            st = d * B_HEADS + hd
            h = _mlstm_chunk(q_ref[:, hd * dh:(hd + 1) * dh], k_ref[:, hd * dh:(hd + 1) * dh],
                             v_ref[:, vo:vo + dh], i_col, f_col, i_row, f_row,
                             c_ref.at[st], n_ref.at[st], m_ref.at[st], rev=bool(d))
            h_ref[:, hd * dh:(hd + 1) * dh] = h


def mlstm(qk, z, g_cols, g_rows, b_gate, bsz, seq, *, chunk=256, name="mlstm"):
    t = qk.shape[0]
    nc = seq // chunk
    assert seq % chunk == 0
    half = B_W // 2
    vcol = COL_VB // half
    assert COL_VB % half == 0
    fw = lambda b, c, s: b * nc + c
    bw = lambda b, c, s: b * nc + (nc - 1 - c)

    def specs(rowf):
        return [pl.BlockSpec((chunk, B_W), lambda b, c, s: (rowf(b, c, s), 0)),
                pl.BlockSpec((chunk, B_W), lambda b, c, s: (rowf(b, c, s), 1)),
                pl.BlockSpec((chunk, half), lambda b, c, s: (rowf(b, c, s), vcol)),
                pl.BlockSpec((chunk, half), lambda b, c, s: (rowf(b, c, s), vcol + 1)),
                pl.BlockSpec((chunk, LANES), lambda b, c, s: (rowf(b, c, s), 0)),
                pl.BlockSpec((16, chunk), lambda b, c, s: (0, rowf(b, c, s)))]

    grid_spec = pltpu.PrefetchScalarGridSpec(
        num_scalar_prefetch=1,
        grid=(bsz, nc),
        in_specs=specs(fw) + specs(bw),
        out_specs=[pl.BlockSpec((chunk, B_W), lambda b, c, s: (fw(b, c, s), 0)),
                   pl.BlockSpec((chunk, B_W), lambda b, c, s: (bw(b, c, s), 0))],
        scratch_shapes=[pltpu.VMEM((2 * B_HEADS, B_HEAD_DIM, B_HEAD_DIM), F32),
                        pltpu.VMEM((2 * B_HEADS, 1, B_HEAD_DIM), F32),
                        pltpu.VMEM((2 * B_HEADS, 1, 1), F32)],
    )
    args = (qk, qk, z, z, g_cols, g_rows)
    return pl.pallas_call(
        _mlstm_kernel,
        grid_spec=grid_spec,
        out_shape=[jax.ShapeDtypeStruct((t, B_W), F32), jax.ShapeDtypeStruct((t, B_W), F32)],
        compiler_params=_params(("arbitrary", "arbitrary"), 48 << 20),
        name=name,
    )(b_gate.astype(F32), *args, *args)


def _mlstm_out_kernel(hf_ref, hb_ref, o0_ref, o1_ref, gain_ref, y_ref):
    dh = B_HEAD_DIM
    for hd in range(B_HEADS):
        sl = slice(hd * dh, (hd + 1) * dh)
        h = hf_ref[:, sl] + hb_ref[:, sl]
        hc = h - jnp.mean(h, axis=1, keepdims=True)
        y = hc * lax.rsqrt(jnp.mean(hc * hc, axis=1, keepdims=True) + EPS) * gain_ref[:, sl]
        o_ref = o0_ref if hd < B_HEADS // 2 else o1_ref
        oo = (hd % (B_HEADS // 2)) * dh
        y_ref[:, sl] = (jax.nn.sigmoid(o_ref[:, oo:oo + dh].astype(F32)) * y).astype(y_ref.dtype)


def mlstm_output(hf, hb, z, gain, *, tm=512, name="mlstm_out"):
    t = hf.shape[0]
    half = B_W // 2
    ocol = COL_OB // half
    assert COL_OB % half == 0 and t % tm == 0
    return pl.pallas_call(
        _mlstm_out_kernel,
        grid=(t // tm,),
        in_specs=[pl.BlockSpec((tm, B_W), lambda i: (i, 0)),
                  pl.BlockSpec((tm, B_W), lambda i: (i, 0)),
                  pl.BlockSpec((tm, half), lambda i: (i, ocol)),
                  pl.BlockSpec((tm, half), lambda i: (i, ocol + 1)),
                  pl.BlockSpec((1, B_W), lambda i: (0, 0))],
        out_specs=pl.BlockSpec((tm, B_W), lambda i: (i, 0)),
        out_shape=jax.ShapeDtypeStruct((t, B_W), BF16),
        compiler_params=_params(("arbitrary",), 32 << 20),
        name=name,
    )(hf, hb, z, z, gain.reshape(1, B_W))


def _up_kernel(te_ref, na_ref, x_ref, wg_ref, wu_ref, h_ref):
    @pl.when(pl.program_id(1) < na_ref[0])
    def _():
        x = x_ref[...]
        g = _dot(x, wg_ref[...].astype(BF16))
        u = _dot(x, wu_ref[...].astype(BF16))
        h_ref[...] = (g * jax.nn.sigmoid(g) * u).astype(h_ref.dtype)

    @pl.when(pl.program_id(1) >= na_ref[0])
    def _():
        h_ref[...] = jnp.zeros_like(h_ref)


def swiglu_up(x, w_gu3, tile_expert, n_active, *, tm=512, tf=512, name="ffn_up"):
    p, d = x.shape
    f = w_gu3.shape[2] // 2
    nt = p // tm
    nj = f // tf
    assert p % tm == 0 and f % tf == 0
    row = lambda i, na: jnp.minimum(i, na[0] - 1)
    grid_spec = pltpu.PrefetchScalarGridSpec(
        num_scalar_prefetch=2,
        grid=(nj, nt),
        in_specs=[pl.BlockSpec((tm, d), lambda j, i, te, na: (row(i, na), 0)),
                  pl.BlockSpec((None, d, tf), lambda j, i, te, na: (te[row(i, na)], 0, j)),
                  pl.BlockSpec((None, d, tf), lambda j, i, te, na: (te[row(i, na)], 0, nj + j))],
        out_specs=pl.BlockSpec((tm, tf), lambda j, i, te, na: (i, j)),
    )
    vm = 2 * _nbytes((tm, d), BF16) + 2 * 3 * _nbytes((d, tf), F32) + 8 * _nbytes((tm, tf), F32) + (4 << 20)
    return pl.pallas_call(
        _up_kernel,
        grid_spec=grid_spec,
        out_shape=jax.ShapeDtypeStruct((p, f), BF16),
        compiler_params=_params(("arbitrary", "arbitrary"), vm),
        name=name,
    )(tile_expert, n_active, x, w_gu3, w_gu3)


def _down_kernel(te_ref, na_ref, h_ref, w_ref, *rest, has_resid):
    o_ref = rest[-1]

    @pl.when(pl.program_id(1) < na_ref[0])
    def _():
        o = _dot(h_ref[...], w_ref[...].astype(BF16))
        if has_resid:
            o = o + rest[0][...]
        o_ref[...] = o

    @pl.when(pl.program_id(1) >= na_ref[0])
    def _():
        o_ref[...] = jnp.zeros_like(o_ref)


def swiglu_down(h, w_d3, tile_expert, n_active, resid=None, *, tm=512, tn=512, name="ffn_down"):
    p, f = h.shape
    d = w_d3.shape[2]
    nt = p // tm
    assert p % tm == 0 and d % tn == 0
    row = lambda i, na: jnp.minimum(i, na[0] - 1)
    in_specs = [pl.BlockSpec((tm, f), lambda n, i, te, na: (row(i, na), 0)),
                pl.BlockSpec((None, f, tn), lambda n, i, te, na: (te[row(i, na)], 0, n))]
    args = [h, w_d3]
    if resid is not None:
        in_specs.append(pl.BlockSpec((tm, tn), lambda n, i, te, na: (row(i, na), n)))
        args.append(resid)
    grid_spec = pltpu.PrefetchScalarGridSpec(
        num_scalar_prefetch=2,
        grid=(d // tn, nt),
        in_specs=in_specs,
        out_specs=pl.BlockSpec((tm, tn), lambda n, i, te, na: (i, n)),
    )
    vm = 2 * _nbytes((tm, f), BF16) + 2 * _nbytes((f, tn), F32) + _nbytes((f, tn), BF16) + 8 * _nbytes((tm, tn), F32) + (4 << 20)
    return pl.pallas_call(
        functools.partial(_down_kernel, has_resid=resid is not None),
        grid_spec=grid_spec,
        out_shape=jax.ShapeDtypeStruct((p, d), F32),
        compiler_params=_params(("arbitrary", "arbitrary"), vm),
        name=name,
    )(tile_expert, n_active, *args)


def _router_kernel(x_ref, g_ref, w_ref, mi_ref, mf_ref, cnt_ref, carry_ref, *, tm):
    @pl.when(pl.program_id(0) == 0)
    def _():
        carry_ref[...] = jnp.zeros_like(carry_ref)

    u = _rms(x_ref[...], g_ref[...])
    logits = jnp.dot(u, w_ref[...], preferred_element_type=F32, precision=lax.Precision.HIGHEST)
    lane = lax.broadcasted_iota(jnp.int32, (tm, LANES), 1)
    logits = jnp.where(lane < N_EXPERTS, logits, -jnp.inf)
    v1 = jnp.max(logits, axis=1, keepdims=True)
    i1 = jnp.min(jnp.where(logits == v1, lane, LANES), axis=1, keepdims=True)
    oh1 = lane == i1
    rest = jnp.where(oh1, -jnp.inf, logits)
    v2 = jnp.max(rest, axis=1, keepdims=True)
    i2 = jnp.min(jnp.where(rest == v2, lane, LANES), axis=1, keepdims=True)
    oh2 = lane == i2
    e = jnp.exp(v2 - v1)
    g1 = 1.0 / (1.0 + e)
    g2 = e * g1
    cnt = (oh1 | oh2).astype(F32)
    r = lax.broadcasted_iota(jnp.int32, (tm, tm), 0)
    c = lax.broadcasted_iota(jnp.int32, (tm, tm), 1)
    before = (c < r).astype(BF16)
    excl = _dot(before, cnt.astype(BF16)) + carry_ref[0:1, :]
    r1 = jnp.sum(jnp.where(oh1, excl, 0.0), axis=1, keepdims=True).astype(jnp.int32)
    r2 = jnp.sum(jnp.where(oh2, excl, 0.0), axis=1, keepdims=True).astype(jnp.int32)
    mi_ref[...] = jnp.where(lane == 0, i1, jnp.where(lane == 1, i2, jnp.where(lane == 2, r1, jnp.where(lane == 3, r2, 0))))
    mf_ref[...] = jnp.where(lane == 0, g1, jnp.where(lane == 1, g2, 0.0))
    carry_ref[...] = carry_ref[...] + jnp.sum(cnt, axis=0, keepdims=True)
    cnt_ref[...] = carry_ref[...]


def moe_router(x, g, w_router3, layer, *, tm=512, name="moe_router"):
    t, d = x.shape
    assert t % tm == 0
    vm = 6 * _nbytes((tm, d), F32) + 4 * _nbytes((d, LANES), F32) + 8 * _nbytes((tm, tm), F32) + (4 << 20)
    return pl.pallas_call(
        functools.partial(_router_kernel, tm=tm),
        grid=(t // tm,),
        in_specs=[pl.BlockSpec((tm, d), lambda i: (i, 0)),
                  pl.BlockSpec((1, d), lambda i: (0, 0)),
                  pl.BlockSpec((None, d, LANES), lambda i: (layer, 0, 0))],
        out_specs=[pl.BlockSpec((tm, LANES), lambda i: (i, 0)),
                   pl.BlockSpec((tm, LANES), lambda i: (i, 0)),
                   pl.BlockSpec((8, LANES), lambda i: (0, 0))],
        out_shape=[jax.ShapeDtypeStruct((t, LANES), jnp.int32),
                   jax.ShapeDtypeStruct((t, LANES), F32),
                   jax.ShapeDtypeStruct((8, LANES), F32)],
        scratch_shapes=[pltpu.VMEM((8, LANES), F32)],
        compiler_params=_params(("arbitrary",), vm),
        name=name,
    )(x, g.reshape(1, d), w_router3)


def _row_copy(src_hbm, dst_vmem, sem, src_row, dst_row):
    return pltpu.make_async_copy(src_hbm.at[pl.ds(src_row, 1), :], dst_vmem.at[pl.ds(dst_row, 1), :], sem)


def _dispatch_kernel(tok_ref, x_hbm, g_ref, o_ref, rows_ref, sem, *, tg):
    def start(r, carry):
        _row_copy(x_hbm, rows_ref, sem, tok_ref[0, 0, r], r).start()
        return carry

    lax.fori_loop(0, tg, start, 0)

    def wait(r, carry):
        _row_copy(x_hbm, rows_ref, sem, 0, r).wait()
        return carry

    lax.fori_loop(0, tg, wait, 0)
    o_ref[...] = _rms(rows_ref[...], g_ref[...]).astype(o_ref.dtype)


def moe_dispatch(x, g, token_of, *, tg=256, name="moe_dispatch"):
    t, d = x.shape
    p = token_of.shape[0]
    assert p % tg == 0
    vm = 3 * _nbytes((tg, d), F32) + 2 * _nbytes((tg, d), BF16) + (4 << 20)
    return pl.pallas_call(
        functools.partial(_dispatch_kernel, tg=tg),
        grid=(p // tg,),
        in_specs=[pl.BlockSpec((1, 1, tg), lambda i: (i, 0, 0), memory_space=pltpu.SMEM),
                  pl.BlockSpec(memory_space=pl.ANY),
                  pl.BlockSpec((1, d), lambda i: (0, 0))],
        out_specs=pl.BlockSpec((tg, d), lambda i: (i, 0)),
        out_shape=jax.ShapeDtypeStruct((p, d), BF16),
        scratch_shapes=[pltpu.VMEM((tg, d), F32), pltpu.SemaphoreType.DMA(())],
        compiler_params=_params(("arbitrary",), vm),
        name=name,
    )(token_of.reshape(p // tg, 1, tg), x, g.reshape(1, d))


def _combine_kernel(p1_ref, p2_ref, o_hbm, mf_ref, x_ref, y_ref, rows_ref, sem, *, tc):
    def start(r, carry):
        _row_copy(o_hbm, rows_ref.at[0], sem, p1_ref[0, 0, r], r).start()
        _row_copy(o_hbm, rows_ref.at[1], sem, p2_ref[0, 0, r], r).start()
        return carry

    lax.fori_loop(0, tc, start, 0)

    def wait(r, carry):
        _row_copy(o_hbm, rows_ref.at[0], sem, 0, r).wait()
        _row_copy(o_hbm, rows_ref.at[1], sem, 0, r).wait()
        return carry

    lax.fori_loop(0, tc, wait, 0)
    mf = mf_ref[...]
    y_ref[...] = x_ref[...] + mf[:, 0:1] * rows_ref[0] + mf[:, 1:2] * rows_ref[1]


def moe_combine(o, pos1, pos2, gates, x, *, tc=256, name="moe_combine"):
    t, d = x.shape
    assert t % tc == 0
    vm = 2 * _nbytes((tc, d), F32) + 6 * _nbytes((tc, d), F32) + (4 << 20)
    idx = pl.BlockSpec((1, 1, tc), lambda i: (i, 0, 0), memory_space=pltpu.SMEM)
    return pl.pallas_call(
        functools.partial(_combine_kernel, tc=tc),
        grid=(t // tc,),
        in_specs=[idx, idx,
                  pl.BlockSpec(memory_space=pl.ANY),
                  pl.BlockSpec((tc, LANES), lambda i: (i, 0)),
                  pl.BlockSpec((tc, d), lambda i: (i, 0))],
        out_specs=pl.BlockSpec((tc, d), lambda i: (i, 0)),
        out_shape=jax.ShapeDtypeStruct((t, d), F32),
        scratch_shapes=[pltpu.VMEM((2, tc, d), F32), pltpu.SemaphoreType.DMA(())],
        compiler_params=_params(("arbitrary",), vm),
        name=name,
    )(pos1.reshape(t // tc, 1, tc), pos2.reshape(t // tc, 1, tc), o, gates, x)


def _rope_tables(seq):
    pos = jnp.arange(seq, dtype=F32)
    inv = jnp.power(ROPE_THETA, -jnp.arange(0, ROT_DIM, 2, dtype=F32) / ROT_DIM)
    ang = pos[:, None] * inv[None, :]
    cos, sin = jnp.cos(ang), jnp.sin(ang)
    half = ROT_DIM // 2
    zeros = jnp.zeros((seq, LANES - ROT_DIM), F32)
    c = jnp.concatenate([cos, cos, jnp.ones_like(zeros)], axis=1)
    s1 = jnp.concatenate([-sin, jnp.zeros((seq, half), F32), zeros], axis=1)
    s2 = jnp.concatenate([jnp.zeros((seq, half), F32), sin, zeros], axis=1)
    return c, s1, s2


def _moe_plan(meta_i, counts, t, tm):
    eid = meta_i[:, 0:2]
    rank = meta_i[:, 2:4]
    cnt = counts[0, :N_EXPERTS].astype(jnp.int32)
    padded = ((cnt + tm - 1) // tm) * tm
    ends = jnp.cumsum(padded)
    starts = ends - padded
    pos = starts[eid] + rank
    p_max = 2 * t + N_EXPERTS * tm
    nt = p_max // tm
    n_active = jnp.maximum(ends[-1] // tm, 1)
    tile_start = jnp.minimum(jnp.arange(nt, dtype=jnp.int32), n_active - 1) * tm
    tile_expert = jnp.minimum(jnp.searchsorted(ends, tile_start, side="right"), N_EXPERTS - 1).astype(jnp.int32)
    tok = jnp.broadcast_to(jnp.arange(t, dtype=jnp.int32)[:, None], (t, 2))
    token_of = jnp.zeros((p_max,), jnp.int32).at[pos.reshape(-1)].set(tok.reshape(-1))
    return pos[:, 0], pos[:, 1], token_of, tile_expert, n_active.reshape(1).astype(jnp.int32)


FFN_TILE = 512


def even_mixer(xs, g, w_in3, gate_w3, conv_w8, b_gate, sink, head_gain, w_out3, j, rope, bsz, seq):
    z = norm_matmul(xs, g, w_in3, j, EVEN_MAIN, rope=rope, seq=seq, n_rope=(A_Q_W + A_KV_W) // HEAD_DIM,
                    n_scale=A_Q_W // HEAD_DIM, scale=HEAD_DIM ** -0.5, name="even_in")
    g_cols = norm_matmul(xs, g, gate_w3, j, LANES, out_dtype=F32, name="even_gates")
    g_rows = g_cols[:, :4 * B_HEADS].T
    ya = window_attention(z, sink, bsz, seq)
    qk = conv_silu(z, conv_w8, bsz, seq)
    hf, hb = mlstm(qk, z, g_cols, g_rows, b_gate, bsz, seq)
    yb = mlstm_output(hf, hb, z, head_gain)
    return matmul_resid([ya, yb], w_out3, j, xs, name="even_out")


def odd_mixer(xs, g, w_in3, w_out3, j, rope, bsz, seq):
    d = xs.shape[1]
    z = norm_matmul(xs, g, w_in3, j, 3 * d, rope=rope, seq=seq, n_rope=2 * C_HEADS, n_scale=C_HEADS,
                    scale=HEAD_DIM ** -0.5, name="odd_in")
    y = dilated_attention(z, bsz, seq)
    return matmul_resid([y], w_out3, j, xs, name="odd_out")


def dense_ffn(xs, g, w_gu3, w_d3, j):
    t = xs.shape[0]
    tiles = jnp.full((t // FFN_TILE,), j, jnp.int32)
    active = jnp.full((1,), t // FFN_TILE, jnp.int32)
    u = rmsnorm(xs, g, BF16, name="ffn_norm")
    h = swiglu_up(u, w_gu3, tiles, active, tm=FFN_TILE, name="ffn_up")
    return swiglu_down(h, w_d3, tiles, active, resid=xs, tm=FFN_TILE, name="ffn_down")


def moe_ffn(xs, g, router_w3, w_gu3, w_d3, j):
    t = xs.shape[0]
    meta_i, gates, counts = moe_router(xs, g, router_w3, j)
    pos1, pos2, token_of, tile_expert, n_active = _moe_plan(meta_i, counts, t, FFN_TILE)
    xd = moe_dispatch(xs, g, token_of)
    h = swiglu_up(xd, w_gu3, tile_expert + j * N_EXPERTS, n_active, tm=FFN_TILE, name="moe_up")
    o = swiglu_down(h, w_d3, tile_expert + j * N_EXPERTS, n_active, tm=FFN_TILE, name="moe_down")
    return moe_combine(o, pos1, pos2, gates, xs)


def kernel(x, mem, ln_mix, ln_xattn, ln_mem, ln_ffn, ln_final, ev_w_in, ev_b_gate, ev_conv, ev_sink, ev_head_norm,
           ev_w_out, ffn_w_gu, ffn_w_down, od_w_in, od_w_out, moe_router, moe_w_gu, moe_w_down, x_wq, x_wkv, x_wo):
    bsz, seq, d = x.shape
    mem_len = mem.shape[1]
    depth = ln_mix.shape[0]
    t = bsz * seq
    rope = _rope_tables(seq)
    xs = x.reshape(t, d)
    memf = mem.reshape(bsz * mem_len, d)
    n_odd = od_w_in.shape[0]
    moe_gu = moe_w_gu.reshape(n_odd * N_EXPERTS, d, 2 * FFN_DIM)
    moe_dn = moe_w_down.reshape(n_odd * N_EXPERTS, FFN_DIM, d)
    router_w = jnp.pad(moe_router, ((0, 0), (0, 0), (0, LANES - N_EXPERTS)))
    gate_w = jnp.pad(ev_w_in[:, :, EVEN_MAIN:], ((0, 0), (0, 0), (0, LANES - 4 * B_HEADS)))
    conv_w8 = jnp.pad(ev_conv, ((0, 0), (0, 8 - B_CONV), (0, 0)))

    for layer in range(depth):
        j = layer // 2
        if layer % 2 == 0:
            xs = even_mixer(xs, ln_mix[layer], ev_w_in, gate_w, conv_w8[j], ev_b_gate[j], ev_sink[j],
                            ev_head_norm[j], ev_w_out, j, rope, bsz, seq)
        else:
            xs = odd_mixer(xs, ln_mix[layer], od_w_in, od_w_out, j, rope, bsz, seq)
        kv = norm_matmul(memf, ln_mem[layer], x_wkv, layer, 2 * X_W, name="mem_kv")
        xs = cross_attention(xs, ln_xattn[layer], x_wq, x_wo, layer, kv, seq, mem_len)
        if layer % 2 == 0:
            xs = dense_ffn(xs, ln_ffn[layer], ffn_w_gu, ffn_w_down, j)
        else:
            xs = moe_ffn(xs, ln_ffn[layer], router_w, moe_gu, moe_dn, j)

    out = rmsnorm(xs, ln_final, F32, name="final_norm")
    return out.reshape(bsz, seq, d)
```

```python
import functools
import math

import jax
import jax.numpy as jnp
from jax import lax
from jax.experimental import pallas as pl
from jax.experimental.pallas import tpu as pltpu

D_MODEL = 2048
HEAD_DIM = 128
A_Q_HEADS = 8
A_KV_HEADS = 2
A_HALF_WINDOW = 128
B_HEADS = 4
B_HEAD_DIM = 256
B_CONV = 5
C_HEADS = D_MODEL // HEAD_DIM
C_PATTERNS = ((128, 1), (512, 4), (2048, 16))
X_HEADS = 4
FFN_DIM = 7168
N_EXPERTS = 8
ROPE_THETA = 500000.0
ROT_DIM = HEAD_DIM // 4
EPS = 1e-6

A_Q_W = A_Q_HEADS * HEAD_DIM
A_KV_W = A_KV_HEADS * HEAD_DIM
B_W = B_HEADS * B_HEAD_DIM
X_W = X_HEADS * HEAD_DIM
EVEN_MAIN = A_Q_W + 2 * A_KV_W + 4 * B_W
COL_KA = A_Q_W
COL_VA = A_Q_W + A_KV_W
COL_QKB = A_Q_W + 2 * A_KV_W
COL_VB = COL_QKB + 2 * B_W
COL_OB = COL_VB + B_W

LANES = 128
V7X_VMEM_BYTES = 64 * 1024 * 1024
VMEM_CAP = V7X_VMEM_BYTES - 8 * 1024 * 1024

NEG = -1e30
BF16 = jnp.bfloat16
F32 = jnp.float32


def _params(sem, vmem_bytes):
    return pltpu.CompilerParams(dimension_semantics=sem, vmem_limit_bytes=int(min(VMEM_CAP, vmem_bytes)))


def _nbytes(shape, dtype):
    return math.prod(shape) * jnp.dtype(dtype).itemsize


def _dot(a, b):
    return jnp.dot(a, b, preferred_element_type=F32)


def _dot_nt(a, b):
    return lax.dot_general(a, b, (((1,), (1,)), ((), ())), preferred_element_type=F32)


def _dot_tn(a, b):
    return lax.dot_general(a, b, (((0,), (0,)), ((), ())), preferred_element_type=F32)


def _rms(x, g):
    return x * lax.rsqrt(jnp.mean(x * x, axis=-1, keepdims=True) + EPS) * g


def _rope_tile(z, c, s1, s2):
    return z * c + pltpu.roll(z, LANES - ROT_DIM // 2, 1) * s1 + pltpu.roll(z, ROT_DIM // 2, 1) * s2


def _norm_mm_kernel(x_ref, g_ref, w_ref, c_ref, s1_ref, s2_ref, o_ref, xn_ref, *, tn, n_rope, n_scale, scale):
    j = pl.program_id(1)

    @pl.when(j == 0)
    def _():
        xn_ref[...] = _rms(x_ref[...], g_ref[...]).astype(BF16)

    z = _dot(xn_ref[...], w_ref[...].astype(BF16))
    heads = tn // LANES
    if n_rope == 0:
        o_ref[...] = z.astype(o_ref.dtype)
        return
    for hh in range(heads):
        zt = z[:, hh * LANES:(hh + 1) * LANES]
        gh = j * heads + hh
        rot = _rope_tile(zt, c_ref[...], s1_ref[...], s2_ref[...])
        zt = jnp.where(gh < n_rope, rot, zt) * jnp.where(gh < n_scale, scale, 1.0)
        o_ref[:, hh * LANES:(hh + 1) * LANES] = zt.astype(o_ref.dtype)


def norm_matmul(x, g, w3, widx, n_out, *, rope=None, n_rope=0, n_scale=0, scale=1.0, seq=None, tm=1024, tn=512,
                out_dtype=BF16, name="norm_mm"):
    m, k = x.shape
    tm = min(tm, m)
    tn = min(tn, n_out)
    assert m % tm == 0 and n_out % tn == 0 and tn % LANES == 0
    if rope is None:
        dummy = jnp.zeros((8, LANES), F32)
        rope = (dummy, dummy, dummy)
        rspec = pl.BlockSpec((8, LANES), lambda i, j: (0, 0))
    else:
        nsb = seq // tm
        assert seq % tm == 0
        rspec = pl.BlockSpec((tm, LANES), lambda i, j: (i % nsb, 0))
    vm = 2 * _nbytes((tm, k), F32) + _nbytes((tm, k), BF16) + 2 * _nbytes((k, tn), F32) + _nbytes((k, tn), BF16) \
        + 2 * _nbytes((tm, tn), out_dtype) + 2 * _nbytes((tm, tn), F32) + 6 * _nbytes((tm, LANES), F32) + (4 << 20)
    return pl.pallas_call(
        functools.partial(_norm_mm_kernel, tn=tn, n_rope=n_rope, n_scale=n_scale, scale=scale),
        grid=(m // tm, n_out // tn),
        in_specs=[pl.BlockSpec((tm, k), lambda i, j: (i, 0)),
                  pl.BlockSpec((1, k), lambda i, j: (0, 0)),
                  pl.BlockSpec((None, k, tn), lambda i, j: (widx, 0, j)),
                  rspec, rspec, rspec],
        out_specs=pl.BlockSpec((tm, tn), lambda i, j: (i, j)),
        out_shape=jax.ShapeDtypeStruct((m, n_out), out_dtype),
        scratch_shapes=[pltpu.VMEM((tm, k), BF16)],
        compiler_params=_params(("arbitrary", "arbitrary"), vm),
        name=name,
    )(x, g.reshape(1, k), w3, *rope)


def _mm_resid_kernel(*refs, n_x):
    xs = refs[:n_x]
    ws = refs[n_x:2 * n_x]
    r_ref = refs[2 * n_x]
    o_ref = refs[2 * n_x + 1]
    acc = r_ref[...]
    for x_ref, w_ref in zip(xs, ws):
        acc = acc + _dot(x_ref[...].astype(BF16), w_ref[...].astype(BF16))
    o_ref[...] = acc


def matmul_resid(xs, w3, widx, resid, *, tm=512, tn=1024, name="mm_resid"):
    m, kx = xs[0].shape
    n = w3.shape[2]
    n_x = len(xs)
    assert w3.shape[1] == n_x * kx and m % tm == 0 and n % tn == 0
    in_specs = [pl.BlockSpec((tm, kx), lambda j, i: (i, 0)) for _ in xs]
    in_specs += [pl.BlockSpec((None, kx, tn), lambda j, i, q=q: (widx, q, j)) for q in range(n_x)]
    in_specs += [pl.BlockSpec((tm, tn), lambda j, i: (i, j))]
    vm = n_x * (2 * _nbytes((tm, kx), xs[0].dtype) + 3 * _nbytes((kx, tn), F32)) + 6 * _nbytes((tm, tn), F32) + (4 << 20)
    return pl.pallas_call(
        functools.partial(_mm_resid_kernel, n_x=n_x),
        grid=(n // tn, m // tm),
        in_specs=in_specs,
        out_specs=pl.BlockSpec((tm, tn), lambda j, i: (i, j)),
        out_shape=jax.ShapeDtypeStruct((m, n), F32),
        compiler_params=_params(("arbitrary", "arbitrary"), vm),
        name=name,
    )(*xs, *([w3] * n_x), resid)


def _rmsnorm_kernel(x_ref, g_ref, o_ref):
    o_ref[...] = _rms(x_ref[...], g_ref[...]).astype(o_ref.dtype)


def rmsnorm(x, g, out_dtype, *, tm=512, name="rmsnorm"):
    m, k = x.shape
    tm = min(tm, m)
    assert m % tm == 0
    vm = 4 * _nbytes((tm, k), F32) + 2 * _nbytes((tm, k), out_dtype) + (4 << 20)
    return pl.pallas_call(
        _rmsnorm_kernel,
        grid=(m // tm,),
        in_specs=[pl.BlockSpec((tm, k), lambda i: (i, 0)), pl.BlockSpec((1, k), lambda i: (0, 0))],
        out_specs=pl.BlockSpec((tm, k), lambda i: (i, 0)),
        out_shape=jax.ShapeDtypeStruct((m, k), out_dtype),
        compiler_params=_params(("arbitrary",), vm),
        name=name,
    )(x, g.reshape(1, k))


def _win_attn_kernel(sink_ref, q_ref, kp_ref, kc_ref, kn_ref, vp_ref, vc_ref, vn_ref, o_ref, *, tq, seq, grp):
    i = pl.program_id(1)
    kv = pl.program_id(2)
    hw = A_HALF_WINDOW
    k = jnp.concatenate([kp_ref[...], kc_ref[...], kn_ref[...]], axis=0)
    v = jnp.concatenate([vp_ref[...], vc_ref[...], vn_ref[...]], axis=0)
    wk = tq + 2 * hw
    qpos = i * tq + lax.broadcasted_iota(jnp.int32, (tq, wk), 0)
    kpos = i * tq - hw + lax.broadcasted_iota(jnp.int32, (tq, wk), 1)
    valid = (jnp.abs(kpos - qpos) <= hw) & (kpos >= 0) & (kpos < seq)
    for g in range(grp):
        sk = sink_ref[kv * grp + g]
        s = _dot_nt(q_ref[:, g * HEAD_DIM:(g + 1) * HEAD_DIM], k)
        s = jnp.where(valid, s, NEG)
        m = jnp.maximum(jnp.max(s, axis=1, keepdims=True), sk)
        p = jnp.exp(s - m)
        den = jnp.sum(p, axis=1, keepdims=True) + jnp.exp(sk - m)
        o = _dot(p.astype(BF16), v) / den
        o_ref[:, g * HEAD_DIM:(g + 1) * HEAD_DIM] = o.astype(o_ref.dtype)


def window_attention(z, sink, bsz, seq, *, tq=256, name="win_attn"):
    t = z.shape[0]
    hw = A_HALF_WINDOW
    grp = A_Q_HEADS // A_KV_HEADS
    nq = seq // tq
    r = tq // hw
    nhb = seq // hw
    qw = grp * HEAD_DIM
    kcol = COL_KA // HEAD_DIM
    vcol = COL_VA // HEAD_DIM
    cur = lambda col: pl.BlockSpec((tq, HEAD_DIM), lambda b, i, kv, s: (b * nq + i, col + kv))
    prev = lambda col: pl.BlockSpec((hw, HEAD_DIM), lambda b, i, kv, s: (b * nhb + jnp.maximum(i * r - 1, 0), col + kv))
    nxt = lambda col: pl.BlockSpec((hw, HEAD_DIM), lambda b, i, kv, s: (b * nhb + jnp.minimum((i + 1) * r, nhb - 1), col + kv))
    grid_spec = pltpu.PrefetchScalarGridSpec(
        num_scalar_prefetch=1,
        grid=(bsz, nq, A_KV_HEADS),
        in_specs=[pl.BlockSpec((tq, qw), lambda b, i, kv, s: (b * nq + i, kv)),
                  prev(kcol), cur(kcol), nxt(kcol), prev(vcol), cur(vcol), nxt(vcol)],
        out_specs=pl.BlockSpec((tq, qw), lambda b, i, kv, s: (b * nq + i, kv)),
    )
    return pl.pallas_call(
        functools.partial(_win_attn_kernel, tq=tq, seq=seq, grp=grp),
        grid_spec=grid_spec,
        out_shape=jax.ShapeDtypeStruct((t, A_Q_W), BF16),
        compiler_params=_params(("arbitrary",) * 3, 32 << 20),
        name=name,
    )(sink.astype(F32), z, z, z, z, z, z, z)


def _dilated_bias(tq, wk, reach):
    row = lax.broadcasted_iota(jnp.int32, (tq, wk), 0)
    col = lax.broadcasted_iota(jnp.int32, (tq, wk), 1)
    d = col - reach - row
    ad = jnp.abs(d)
    mult = jnp.zeros((tq, wk), jnp.int32)
    for window, dil in C_PATTERNS:
        mult = mult + ((ad <= window // 2) & ((d & (dil - 1)) == 0)).astype(jnp.int32)
    bias = jnp.where(mult == 1, 0.0, jnp.where(mult == 2, 1.0, math.log2(3.0)))
    return jnp.where(mult == 0, NEG, bias).astype(F32)


def _dil_attn_kernel(q_ref, k_ref, v_ref, o_ref, kp_ref, vp_ref, bias_ref, *, tq, seq, reach, nh, rb):
    b = pl.program_id(0)
    hg = pl.program_id(1)
    i = pl.program_id(2)
    wk = tq + 2 * reach
    hd = HEAD_DIM

    @pl.when((b == 0) & (hg == 0) & (i == 0))
    def _():
        bias_ref[...] = _dilated_bias(tq, wk, reach)
        kp_ref[...] = jnp.zeros_like(kp_ref)
        vp_ref[...] = jnp.zeros_like(vp_ref)
        lane = lax.broadcasted_iota(jnp.int32, (seq, hd), 1)
        ones_col = jnp.where(lane == 0, 1.0, 0.0).astype(BF16)
        for h in range(nh):
            vp_ref[h, reach:reach + seq, hd:2 * hd] = ones_col

    @pl.when(i == 0)
    def _():
        for h in range(nh):
            kp_ref[h, reach:reach + seq, :] = k_ref[:, h * hd:(h + 1) * hd]
            vp_ref[h, reach:reach + seq, 0:hd] = v_ref[:, h * hd:(h + 1) * hd]

    start = pl.multiple_of(i * tq, tq)

    def attend(edge):
        if edge:
            kpos = i * tq - reach + lax.broadcasted_iota(jnp.int32, (1, wk), 1)
            colbias = jnp.where((kpos >= 0) & (kpos < seq), 0.0, NEG).astype(F32)
        for h in range(nh):
            ks = kp_ref[h, pl.ds(start, wk), :]
            vs = vp_ref[h, pl.ds(start, wk), :]
            for r0 in range(0, tq, rb):
                s = _dot_nt(q_ref[r0:r0 + rb, h * hd:(h + 1) * hd], ks) + bias_ref[r0:r0 + rb, :]
                if edge:
                    s = s + colbias
                m = jnp.max(s, axis=1, keepdims=True)
                p = jnp.exp2(s - m).astype(BF16)
                pv = _dot(p, vs)
                o_ref[r0:r0 + rb, h * hd:(h + 1) * hd] = (pv[:, 0:hd] / pv[:, hd:hd + 1]).astype(o_ref.dtype)

    is_edge = (i * tq < reach) | (i * tq + tq + reach > seq)
    pl.when(is_edge)(lambda: attend(True))
    pl.when(jnp.logical_not(is_edge))(lambda: attend(False))


def dilated_attention(z, bsz, seq, *, tq=256, nh=2, rb=128, name="dil_attn"):
    t = z.shape[0]
    reach = max(w // 2 for w, _ in C_PATTERNS)
    nq = seq // tq
    wk = tq + 2 * reach
    ng = C_HEADS // nh
    gw = nh * HEAD_DIM
    assert seq % tq == 0 and seq >= wk and C_HEADS % nh == 0
    vm = 4 * _nbytes((seq, gw), BF16) + 3 * nh * _nbytes((seq + 2 * reach, HEAD_DIM), BF16) \
        + (1 + 3 * nh) * _nbytes((tq, wk), F32) + (8 << 20)
    return pl.pallas_call(
        functools.partial(_dil_attn_kernel, tq=tq, seq=seq, reach=reach, nh=nh, rb=rb),
        grid=(bsz, ng, nq),
        in_specs=[pl.BlockSpec((tq, gw), lambda b, g, i: (b * nq + i, g)),
                  pl.BlockSpec((seq, gw), lambda b, g, i: (b, ng + g)),
                  pl.BlockSpec((seq, gw), lambda b, g, i: (b, 2 * ng + g))],
        out_specs=pl.BlockSpec((tq, gw), lambda b, g, i: (b * nq + i, g)),
        out_shape=jax.ShapeDtypeStruct((t, D_MODEL), BF16),
        scratch_shapes=[pltpu.VMEM((nh, seq + 2 * reach, HEAD_DIM), BF16),
                        pltpu.VMEM((nh, seq + 2 * reach, 2 * HEAD_DIM), BF16),
                        pltpu.VMEM((tq, wk), F32)],
        compiler_params=_params(("arbitrary",) * 3, vm),
        name=name,
    )(z, z, z)


def _xattn_kernel(x_ref, g_ref, wq_ref, kv_ref, wo_ref, o_ref, wq_s, wo_s):
    @pl.when(pl.program_id(0) == 0)
    def _():
        wq_s[...] = wq_ref[...].astype(BF16)
        wo_s[...] = wo_ref[...].astype(BF16)

    x = x_ref[...]
    xn = _rms(x, g_ref[...]).astype(BF16)
    q = (_dot(xn, wq_s[...]) * (HEAD_DIM ** -0.5)).astype(BF16)
    kv = kv_ref[...]
    outs = []
    for h in range(X_HEADS):
        k = kv[:, h * HEAD_DIM:(h + 1) * HEAD_DIM]
        v = kv[:, X_W + h * HEAD_DIM:X_W + (h + 1) * HEAD_DIM]
        s = _dot_nt(q[:, h * HEAD_DIM:(h + 1) * HEAD_DIM], k)
        m = jnp.max(s, axis=1, keepdims=True)
        p = jnp.exp(s - m)
        den = jnp.sum(p, axis=1, keepdims=True)
        outs.append((_dot(p.astype(BF16), v) / den).astype(BF16))
    o = jnp.concatenate(outs, axis=1)
    o_ref[...] = x + _dot(o, wo_s[...])


def cross_attention(x, g, wq3, wo3, layer, kv, seq, mem_len, *, tm=512, name="xattn"):
    t, d = x.shape
    nsb = seq // tm
    assert seq % tm == 0
    vm = 4 * _nbytes((tm, d), F32) + 3 * _nbytes((d, X_W), F32) * 2 + 6 * _nbytes((tm, d), F32) + (8 << 20)
    return pl.pallas_call(
        _xattn_kernel,
        grid=(t // tm,),
        in_specs=[pl.BlockSpec((tm, d), lambda i: (i, 0)),
                  pl.BlockSpec((1, d), lambda i: (0, 0)),
                  pl.BlockSpec((None, d, X_W), lambda i: (layer, 0, 0)),
                  pl.BlockSpec((mem_len, 2 * X_W), lambda i: (i // nsb, 0)),
                  pl.BlockSpec((None, X_W, d), lambda i: (layer, 0, 0))],
        out_specs=pl.BlockSpec((tm, d), lambda i: (i, 0)),
        out_shape=jax.ShapeDtypeStruct((t, d), F32),
        scratch_shapes=[pltpu.VMEM((d, X_W), BF16), pltpu.VMEM((X_W, d), BF16)],
        compiler_params=_params(("arbitrary",), vm),
        name=name,
    )(x, g.reshape(1, d), wq3, kv, wo3)


def _conv_kernel(xp_ref, xc_ref, xn_ref, w_ref, o_ref, *, ts, ns, halo, q_scale, n_q_blocks):
    i = pl.program_id(1)
    j = pl.program_id(2)
    pad = B_CONV // 2
    xp = jnp.where(i > 0, xp_ref[halo - pad:, :].astype(F32), 0.0)
    xn = jnp.where(i < ns - 1, xn_ref[:pad, :].astype(F32), 0.0)
    xx = jnp.concatenate([xp, xc_ref[...].astype(F32), xn], axis=0)
    w = w_ref[...]
    acc = xx[0:ts, :] * w[0:1, :]
    for tap in range(1, B_CONV):
        acc = acc + xx[tap:tap + ts, :] * w[tap:tap + 1, :]
    y = acc * jax.nn.sigmoid(acc)
    y = y * jnp.where(j < n_q_blocks, q_scale, 1.0)
    o_ref[...] = y.astype(o_ref.dtype)


def conv_silu(z, conv_w8, bsz, seq, *, ts=512, tc=512, name="conv_silu"):
    t = z.shape[0]
    halo = 16
    ns = seq // ts
    nh = seq // halo
    cb = COL_QKB // tc
    assert COL_QKB % tc == 0 and seq % ts == 0 and B_W % tc == 0
    return pl.pallas_call(
        functools.partial(_conv_kernel, ts=ts, ns=ns, halo=halo, q_scale=B_HEAD_DIM ** -0.5, n_q_blocks=B_W // tc),
        grid=(bsz, ns, 2 * B_W // tc),
        in_specs=[pl.BlockSpec((halo, tc), lambda b, i, j: (b * nh + jnp.maximum(i * (ts // halo) - 1, 0), cb + j)),
                  pl.BlockSpec((ts, tc), lambda b, i, j: (b * ns + i, cb + j)),
                  pl.BlockSpec((halo, tc), lambda b, i, j: (b * nh + jnp.minimum((i + 1) * (ts // halo), nh - 1), cb + j)),
                  pl.BlockSpec((8, tc), lambda b, i, j: (0, j))],
        out_specs=pl.BlockSpec((ts, tc), lambda b, i, j: (b * ns + i, j)),
        out_shape=jax.ShapeDtypeStruct((t, 2 * B_W), BF16),
        compiler_params=_params(("arbitrary",) * 3, 32 << 20),
        name=name,
    )(z, z, z, conv_w8)


def _log_sigmoid(x):
    return jnp.minimum(x, 0.0) - jnp.log(1.0 + jnp.exp(-jnp.abs(x)))


def _mlstm_chunk(q, k, v, i_col, f_col, i_row, f_row, c_ref, n_ref, m_ref, rev):
    L = q.shape[0]
    logf_c = _log_sigmoid(f_col)
    logf_r = _log_sigmoid(f_row)
    row = lax.broadcasted_iota(jnp.int32, (L, L), 0)
    col = lax.broadcasted_iota(jnp.int32, (L, L), 1)
    causal = (col >= row) if rev else (col <= row)
    causal_t = (row >= col) if rev else (row <= col)
    b_col = jnp.sum(jnp.where(causal, logf_r, 0.0), axis=1, keepdims=True)
    b_row = jnp.sum(jnp.where(causal_t, logf_c, 0.0), axis=0, keepdims=True)
    b_all = jnp.sum(logf_r, axis=1, keepdims=True)
    m_st = m_ref[...]
    c_st = c_ref[...]
    n_st = n_ref[...]
    logd = jnp.where(causal, b_col - b_row + i_row, NEG)
    m_inter = b_col + m_st
    mt = jnp.maximum(m_inter, jnp.max(logd, axis=1, keepdims=True))
    s = _dot_nt(q, k) * jnp.exp(logd - mt)
    sc = jnp.exp(m_inter - mt)
    num = _dot(s.astype(BF16), v) + sc * _dot(q, c_st.astype(BF16))
    den = jnp.sum(s, axis=1, keepdims=True) + sc * jnp.sum(q.astype(F32) * n_st, axis=1, keepdims=True)
    h = num / jnp.maximum(jnp.abs(den), jnp.exp(-mt))
    logw = b_all - b_col + i_col
    m_new = jnp.maximum(b_all + m_st, jnp.max(logw, axis=0, keepdims=True))
    wgt = jnp.exp(logw - m_new)
    dec = jnp.exp(b_all + m_st - m_new)
    kw = k.astype(F32) * wgt
    c_ref[...] = dec * c_st + _dot_tn(kw.astype(BF16), v)
    n_ref[...] = dec * n_st + jnp.sum(kw, axis=0, keepdims=True)
    m_ref[...] = m_new
    return h


def _mlstm_kernel(bias_ref, qf_ref, kf_ref, vf0_ref, vf1_ref, gcf_ref, grf_ref,
                  qb_ref, kb_ref, vb0_ref, vb1_ref, gcb_ref, grb_ref,
                  hf_ref, hb_ref, c_ref, n_ref, m_ref):
    @pl.when(pl.program_id(1) == 0)
    def _():
        c_ref[...] = jnp.zeros_like(c_ref)
        n_ref[...] = jnp.zeros_like(n_ref)
        m_ref[...] = jnp.zeros_like(m_ref)

    lane = lax.broadcasted_iota(jnp.int32, (1, LANES), 1)
    dh = B_HEAD_DIM
    for d, (q_ref, k_ref, v0_ref, v1_ref, gc_ref, gr_ref, h_ref) in enumerate((
            (qf_ref, kf_ref, vf0_ref, vf1_ref, gcf_ref, grf_ref, hf_ref),
            (qb_ref, kb_ref, vb0_ref, vb1_ref, gcb_ref, grb_ref, hb_ref))):
        gc = gc_ref[...]
        for hd in range(B_HEADS):
            ci = 2 * d * B_HEADS + hd
            cf = ci + B_HEADS
            i_col = jnp.sum(jnp.where(lane == ci, gc, 0.0), axis=1, keepdims=True) + bias_ref[ci]
            f_col = jnp.sum(jnp.where(lane == cf, gc, 0.0), axis=1, keepdims=True) + bias_ref[cf]
            i_row = gr_ref[ci:ci + 1, :] + bias_ref[ci]
            f_row = gr_ref[cf:cf + 1, :] + bias_ref[cf]
            v_ref = v0_ref if hd < B_HEADS // 2 else v1_ref
            vo = (hd % (B_HEADS // 2)) * dh
            st = d * B_HEADS + hd
            h = _mlstm_chunk(q_ref[:, hd * dh:(hd + 1) * dh], k_ref[:, hd * dh:(hd + 1) * dh],
                             v_ref[:, vo:vo + dh], i_col, f_col, i_row, f_row,
                             c_ref.at[st], n_ref.at[st], m_ref.at[st], rev=bool(d))
            h_ref[:, hd * dh:(hd + 1) * dh] = h


def mlstm(qk, z, g_cols, g_rows, b_gate, bsz, seq, *, chunk=256, name="mlstm"):
    t = qk.shape[0]
    nc = seq // chunk
    assert seq % chunk == 0
    half = B_W // 2
    vcol = COL_VB // half
    assert COL_VB % half == 0
    fw = lambda b, c, s: b * nc + c
    bw = lambda b, c, s: b * nc + (nc - 1 - c)

    def specs(rowf):
        return [pl.BlockSpec((chunk, B_W), lambda b, c, s: (rowf(b, c, s), 0)),
                pl.BlockSpec((chunk, B_W), lambda b, c, s: (rowf(b, c, s), 1)),
                pl.BlockSpec((chunk, half), lambda b, c, s: (rowf(b, c, s), vcol)),
                pl.BlockSpec((chunk, half), lambda b, c, s: (rowf(b, c, s), vcol + 1)),
                pl.BlockSpec((chunk, LANES), lambda b, c, s: (rowf(b, c, s), 0)),
                pl.BlockSpec((16, chunk), lambda b, c, s: (0, rowf(b, c, s)))]

    grid_spec = pltpu.PrefetchScalarGridSpec(
        num_scalar_prefetch=1,
        grid=(bsz, nc),
        in_specs=specs(fw) + specs(bw),
        out_specs=[pl.BlockSpec((chunk, B_W), lambda b, c, s: (fw(b, c, s), 0)),
                   pl.BlockSpec((chunk, B_W), lambda b, c, s: (bw(b, c, s), 0))],
        scratch_shapes=[pltpu.VMEM((2 * B_HEADS, B_HEAD_DIM, B_HEAD_DIM), F32),
                        pltpu.VMEM((2 * B_HEADS, 1, B_HEAD_DIM), F32),
                        pltpu.VMEM((2 * B_HEADS, 1, 1), F32)],
    )
    args = (qk, qk, z, z, g_cols, g_rows)
    return pl.pallas_call(
        _mlstm_kernel,
        grid_spec=grid_spec,
        out_shape=[jax.ShapeDtypeStruct((t, B_W), F32), jax.ShapeDtypeStruct((t, B_W), F32)],
        compiler_params=_params(("arbitrary", "arbitrary"), 48 << 20),
        name=name,
    )(b_gate.astype(F32), *args, *args)


def _mlstm_out_kernel(hf_ref, hb_ref, o0_ref, o1_ref, gain_ref, y_ref):
    dh = B_HEAD_DIM
    for hd in range(B_HEADS):
        sl = slice(hd * dh, (hd + 1) * dh)
        h = hf_ref[:, sl] + hb_ref[:, sl]
        hc = h - jnp.mean(h, axis=1, keepdims=True)
        y = hc * lax.rsqrt(jnp.mean(hc * hc, axis=1, keepdims=True) + EPS) * gain_ref[:, sl]
        o_ref = o0_ref if hd < B_HEADS // 2 else o1_ref
        oo = (hd % (B_HEADS // 2)) * dh
        y_ref[:, sl] = (jax.nn.sigmoid(o_ref[:, oo:oo + dh].astype(F32)) * y).astype(y_ref.dtype)


def mlstm_output(hf, hb, z, gain, *, tm=512, name="mlstm_out"):
    t = hf.shape[0]
    half = B_W // 2
    ocol = COL_OB // half
    assert COL_OB % half == 0 and t % tm == 0
    return pl.pallas_call(
        _mlstm_out_kernel,
        grid=(t // tm,),
        in_specs=[pl.BlockSpec((tm, B_W), lambda i: (i, 0)),
                  pl.BlockSpec((tm, B_W), lambda i: (i, 0)),
                  pl.BlockSpec((tm, half), lambda i: (i, ocol)),
                  pl.BlockSpec((tm, half), lambda i: (i, ocol + 1)),
                  pl.BlockSpec((1, B_W), lambda i: (0, 0))],
        out_specs=pl.BlockSpec((tm, B_W), lambda i: (i, 0)),
        out_shape=jax.ShapeDtypeStruct((t, B_W), BF16),
        compiler_params=_params(("arbitrary",), 32 << 20),
        name=name,
    )(hf, hb, z, z, gain.reshape(1, B_W))


def _up_kernel(te_ref, na_ref, x_ref, wg_ref, wu_ref, h_ref):
    @pl.when(pl.program_id(1) < na_ref[0])
    def _():
        x = x_ref[...]
        g = _dot(x, wg_ref[...].astype(BF16))
        u = _dot(x, wu_ref[...].astype(BF16))
        h_ref[...] = (g * jax.nn.sigmoid(g) * u).astype(h_ref.dtype)

    @pl.when(pl.program_id(1) >= na_ref[0])
    def _():
        h_ref[...] = jnp.zeros_like(h_ref)


def swiglu_up(x, w_gu3, tile_expert, n_active, *, tm=512, tf=512, name="ffn_up"):
    p, d = x.shape
    f = w_gu3.shape[2] // 2
    nt = p // tm
    nj = f // tf
    assert p % tm == 0 and f % tf == 0
    row = lambda i, na: jnp.minimum(i, na[0] - 1)
    grid_spec = pltpu.PrefetchScalarGridSpec(
        num_scalar_prefetch=2,
        grid=(nj, nt),
        in_specs=[pl.BlockSpec((tm, d), lambda j, i, te, na: (row(i, na), 0)),
                  pl.BlockSpec((None, d, tf), lambda j, i, te, na: (te[row(i, na)], 0, j)),
                  pl.BlockSpec((None, d, tf), lambda j, i, te, na: (te[row(i, na)], 0, nj + j))],
        out_specs=pl.BlockSpec((tm, tf), lambda j, i, te, na: (i, j)),
    )
    vm = 2 * _nbytes((tm, d), BF16) + 2 * 3 * _nbytes((d, tf), F32) + 8 * _nbytes((tm, tf), F32) + (4 << 20)
    return pl.pallas_call(
        _up_kernel,
        grid_spec=grid_spec,
        out_shape=jax.ShapeDtypeStruct((p, f), BF16),
        compiler_params=_params(("arbitrary", "arbitrary"), vm),
        name=name,
    )(tile_expert, n_active, x, w_gu3, w_gu3)


def _down_kernel(te_ref, na_ref, h_ref, w_ref, *rest, has_resid):
    o_ref = rest[-1]

    @pl.when(pl.program_id(1) < na_ref[0])
    def _():
        o = _dot(h_ref[...], w_ref[...].astype(BF16))
        if has_resid:
            o = o + rest[0][...]
        o_ref[...] = o

    @pl.when(pl.program_id(1) >= na_ref[0])
    def _():
        o_ref[...] = jnp.zeros_like(o_ref)


def swiglu_down(h, w_d3, tile_expert, n_active, resid=None, *, tm=512, tn=512, name="ffn_down"):
    p, f = h.shape
    d = w_d3.shape[2]
    nt = p // tm
    assert p % tm == 0 and d % tn == 0
    row = lambda i, na: jnp.minimum(i, na[0] - 1)
    in_specs = [pl.BlockSpec((tm, f), lambda n, i, te, na: (row(i, na), 0)),
                pl.BlockSpec((None, f, tn), lambda n, i, te, na: (te[row(i, na)], 0, n))]
    args = [h, w_d3]
    if resid is not None:
        in_specs.append(pl.BlockSpec((tm, tn), lambda n, i, te, na: (row(i, na), n)))
        args.append(resid)
    grid_spec = pltpu.PrefetchScalarGridSpec(
        num_scalar_prefetch=2,
        grid=(d // tn, nt),
        in_specs=in_specs,
        out_specs=pl.BlockSpec((tm, tn), lambda n, i, te, na: (i, n)),
    )
    vm = 2 * _nbytes((tm, f), BF16) + 2 * _nbytes((f, tn), F32) + _nbytes((f, tn), BF16) + 8 * _nbytes((tm, tn), F32) + (4 << 20)
    return pl.pallas_call(
        functools.partial(_down_kernel, has_resid=resid is not None),
        grid_spec=grid_spec,
        out_shape=jax.ShapeDtypeStruct((p, d), F32),
        compiler_params=_params(("arbitrary", "arbitrary"), vm),
        name=name,
    )(tile_expert, n_active, *args)


def _router_kernel(x_ref, g_ref, w_ref, mi_ref, mf_ref, cnt_ref, carry_ref, *, tm):
    @pl.when(pl.program_id(0) == 0)
    def _():
        carry_ref[...] = jnp.zeros_like(carry_ref)

    u = _rms(x_ref[...], g_ref[...])
    logits = jnp.dot(u, w_ref[...], preferred_element_type=F32, precision=lax.Precision.HIGHEST)
    lane = lax.broadcasted_iota(jnp.int32, (tm, LANES), 1)
    logits = jnp.where(lane < N_EXPERTS, logits, -jnp.inf)
    v1 = jnp.max(logits, axis=1, keepdims=True)
    i1 = jnp.min(jnp.where(logits == v1, lane, LANES), axis=1, keepdims=True)
    oh1 = lane == i1
    rest = jnp.where(oh1, -jnp.inf, logits)
    v2 = jnp.max(rest, axis=1, keepdims=True)
    i2 = jnp.min(jnp.where(rest == v2, lane, LANES), axis=1, keepdims=True)
    oh2 = lane == i2
    e = jnp.exp(v2 - v1)
    g1 = 1.0 / (1.0 + e)
    g2 = e * g1
    cnt = (oh1 | oh2).astype(F32)
    r = lax.broadcasted_iota(jnp.int32, (tm, tm), 0)
    c = lax.broadcasted_iota(jnp.int32, (tm, tm), 1)
    before = (c < r).astype(BF16)
    excl = _dot(before, cnt.astype(BF16)) + carry_ref[0:1, :]
    r1 = jnp.sum(jnp.where(oh1, excl, 0.0), axis=1, keepdims=True).astype(jnp.int32)
    r2 = jnp.sum(jnp.where(oh2, excl, 0.0), axis=1, keepdims=True).astype(jnp.int32)
    mi_ref[...] = jnp.where(lane == 0, i1, jnp.where(lane == 1, i2, jnp.where(lane == 2, r1, jnp.where(lane == 3, r2, 0))))
    mf_ref[...] = jnp.where(lane == 0, g1, jnp.where(lane == 1, g2, 0.0))
    carry_ref[...] = carry_ref[...] + jnp.sum(cnt, axis=0, keepdims=True)
    cnt_ref[...] = carry_ref[...]


def moe_router(x, g, w_router3, layer, *, tm=512, name="moe_router"):
    t, d = x.shape
    assert t % tm == 0
    vm = 6 * _nbytes((tm, d), F32) + 4 * _nbytes((d, LANES), F32) + 8 * _nbytes((tm, tm), F32) + (4 << 20)
    return pl.pallas_call(
        functools.partial(_router_kernel, tm=tm),
        grid=(t // tm,),
        in_specs=[pl.BlockSpec((tm, d), lambda i: (i, 0)),
                  pl.BlockSpec((1, d), lambda i: (0, 0)),
                  pl.BlockSpec((None, d, LANES), lambda i: (layer, 0, 0))],
        out_specs=[pl.BlockSpec((tm, LANES), lambda i: (i, 0)),
                   pl.BlockSpec((tm, LANES), lambda i: (i, 0)),
                   pl.BlockSpec((8, LANES), lambda i: (0, 0))],
        out_shape=[jax.ShapeDtypeStruct((t, LANES), jnp.int32),
                   jax.ShapeDtypeStruct((t, LANES), F32),
                   jax.ShapeDtypeStruct((8, LANES), F32)],
        scratch_shapes=[pltpu.VMEM((8, LANES), F32)],
        compiler_params=_params(("arbitrary",), vm),
        name=name,
    )(x, g.reshape(1, d), w_router3)


def _row_copy(src_hbm, dst_vmem, sem, src_row, dst_row):
    return pltpu.make_async_copy(src_hbm.at[pl.ds(src_row, 1), :], dst_vmem.at[pl.ds(dst_row, 1), :], sem)


GATHER_UNROLL = 8


def _start_rows(src_hbm, idx_refs, dst_refs, sem, n):
    def body(r, carry):
        for idx_ref, dst in zip(idx_refs, dst_refs):
            _row_copy(src_hbm, dst, sem, idx_ref[0, 0, r], r).start()
        return carry

    lax.fori_loop(0, n, body, 0, unroll=GATHER_UNROLL)


def _wait_rows(src_hbm, dst_refs, sem, n):
    def body(r, carry):
        for dst in dst_refs:
            _row_copy(src_hbm, dst, sem, 0, r).wait()
        return carry

    lax.fori_loop(0, n, body, 0, unroll=GATHER_UNROLL)


def _dispatch_kernel(tok_ref, tok_next_ref, x_hbm, g_ref, o_ref, rows_ref, sems, *, tg):
    i = pl.program_id(0)
    slot = i % 2

    @pl.when(i == 0)
    def _():
        _start_rows(x_hbm, [tok_ref], [rows_ref.at[0]], sems.at[0], tg)

    @pl.when(i + 1 < pl.num_programs(0))
    def _():
        _start_rows(x_hbm, [tok_next_ref], [rows_ref.at[1 - slot]], sems.at[1 - slot], tg)

    _wait_rows(x_hbm, [rows_ref.at[slot]], sems.at[slot], tg)
    o_ref[...] = _rms(rows_ref[slot], g_ref[...]).astype(o_ref.dtype)


def moe_dispatch(x, g, token_of, *, tg=256, name="moe_dispatch"):
    t, d = x.shape
    p = token_of.shape[0]
    assert p % tg == 0
    nt = p // tg
    vm = 4 * _nbytes((tg, d), F32) + 2 * _nbytes((tg, d), BF16) + (4 << 20)
    tok3 = token_of.reshape(nt, 1, tg)
    return pl.pallas_call(
        functools.partial(_dispatch_kernel, tg=tg),
        grid=(nt,),
        in_specs=[pl.BlockSpec((1, 1, tg), lambda i: (i, 0, 0), memory_space=pltpu.SMEM),
                  pl.BlockSpec((1, 1, tg), lambda i: (jnp.minimum(i + 1, nt - 1), 0, 0), memory_space=pltpu.SMEM),
                  pl.BlockSpec(memory_space=pl.ANY),
                  pl.BlockSpec((1, d), lambda i: (0, 0))],
        out_specs=pl.BlockSpec((tg, d), lambda i: (i, 0)),
        out_shape=jax.ShapeDtypeStruct((p, d), BF16),
        scratch_shapes=[pltpu.VMEM((2, tg, d), F32), pltpu.SemaphoreType.DMA((2,))],
        compiler_params=_params(("arbitrary",), vm),
        name=name,
    )(tok3, tok3, x, g.reshape(1, d))


def _combine_kernel(p1_ref, p2_ref, p1n_ref, p2n_ref, o_hbm, mf_ref, x_ref, y_ref, rows_ref, sems, *, tc):
    i = pl.program_id(0)
    slot = i % 2
    dst = lambda s: [rows_ref.at[s, 0], rows_ref.at[s, 1]]

    @pl.when(i == 0)
    def _():
        _start_rows(o_hbm, [p1_ref, p2_ref], dst(0), sems.at[0], tc)

    @pl.when(i + 1 < pl.num_programs(0))
    def _():
        _start_rows(o_hbm, [p1n_ref, p2n_ref], dst(1 - slot), sems.at[1 - slot], tc)

    _wait_rows(o_hbm, dst(slot), sems.at[slot], tc)
    mf = mf_ref[...]
    y_ref[...] = x_ref[...] + mf[:, 0:1] * rows_ref[slot, 0] + mf[:, 1:2] * rows_ref[slot, 1]


def moe_combine(o, pos1, pos2, gates, x, *, tc=256, name="moe_combine"):
    t, d = x.shape
    assert t % tc == 0
    nt = t // tc
    vm = 4 * _nbytes((tc, d), F32) + 6 * _nbytes((tc, d), F32) + (4 << 20)
    cur = pl.BlockSpec((1, 1, tc), lambda i: (i, 0, 0), memory_space=pltpu.SMEM)
    nxt = pl.BlockSpec((1, 1, tc), lambda i: (jnp.minimum(i + 1, nt - 1), 0, 0), memory_space=pltpu.SMEM)
    p1 = pos1.reshape(nt, 1, tc)
    p2 = pos2.reshape(nt, 1, tc)
    return pl.pallas_call(
        functools.partial(_combine_kernel, tc=tc),
        grid=(nt,),
        in_specs=[cur, cur, nxt, nxt,
                  pl.BlockSpec(memory_space=pl.ANY),
                  pl.BlockSpec((tc, LANES), lambda i: (i, 0)),
                  pl.BlockSpec((tc, d), lambda i: (i, 0))],
        out_specs=pl.BlockSpec((tc, d), lambda i: (i, 0)),
        out_shape=jax.ShapeDtypeStruct((t, d), F32),
        scratch_shapes=[pltpu.VMEM((2, 2, tc, d), F32), pltpu.SemaphoreType.DMA((2,))],
        compiler_params=_params(("arbitrary",), vm),
        name=name,
    )(p1, p2, p1, p2, o, gates, x)


def _rope_tables(seq):
    pos = jnp.arange(seq, dtype=F32)
    inv = jnp.power(ROPE_THETA, -jnp.arange(0, ROT_DIM, 2, dtype=F32) / ROT_DIM)
    ang = pos[:, None] * inv[None, :]
    cos, sin = jnp.cos(ang), jnp.sin(ang)
    half = ROT_DIM // 2
    zeros = jnp.zeros((seq, LANES - ROT_DIM), F32)
    c = jnp.concatenate([cos, cos, jnp.ones_like(zeros)], axis=1)
    s1 = jnp.concatenate([-sin, jnp.zeros((seq, half), F32), zeros], axis=1)
    s2 = jnp.concatenate([jnp.zeros((seq, half), F32), sin, zeros], axis=1)
    return c, s1, s2


def _moe_plan(meta_i, counts, t, tm):
    eid = meta_i[:, 0:2]
    rank = meta_i[:, 2:4]
    cnt = counts[0, :N_EXPERTS].astype(jnp.int32)
    padded = ((cnt + tm - 1) // tm) * tm
    ends = jnp.cumsum(padded)
    starts = ends - padded
    pos = starts[eid] + rank
    p_max = 2 * t + N_EXPERTS * tm
    nt = p_max // tm
    n_active = jnp.maximum(ends[-1] // tm, 1)
    tile_start = jnp.minimum(jnp.arange(nt, dtype=jnp.int32), n_active - 1) * tm
    tile_expert = jnp.minimum(jnp.searchsorted(ends, tile_start, side="right"), N_EXPERTS - 1).astype(jnp.int32)
    tok = jnp.broadcast_to(jnp.arange(t, dtype=jnp.int32)[:, None], (t, 2))
    token_of = jnp.zeros((p_max,), jnp.int32).at[pos.reshape(-1)].set(tok.reshape(-1))
    return pos[:, 0], pos[:, 1], token_of, tile_expert, n_active.reshape(1).astype(jnp.int32)


FFN_TILE = 512
DENSE_UP_TILE = 1024


def even_mixer(xs, g, w_in3, gate_w3, conv_w8, b_gate, sink, head_gain, w_out3, j, rope, bsz, seq):
    z = norm_matmul(xs, g, w_in3, j, EVEN_MAIN, rope=rope, seq=seq, n_rope=(A_Q_W + A_KV_W) // HEAD_DIM,
                    n_scale=A_Q_W // HEAD_DIM, scale=HEAD_DIM ** -0.5, name="even_in")
    g_cols = norm_matmul(xs, g, gate_w3, j, LANES, out_dtype=F32, name="even_gates")
    g_rows = g_cols[:, :4 * B_HEADS].T
    ya = window_attention(z, sink, bsz, seq)
    qk = conv_silu(z, conv_w8, bsz, seq)
    hf, hb = mlstm(qk, z, g_cols, g_rows, b_gate, bsz, seq)
    yb = mlstm_output(hf, hb, z, head_gain)
    return matmul_resid([ya, yb], w_out3, j, xs, name="even_out")


def odd_mixer(xs, g, w_in3, w_out3, j, rope, bsz, seq):
    d = xs.shape[1]
    z = norm_matmul(xs, g, w_in3, j, 3 * d, rope=rope, seq=seq, n_rope=2 * C_HEADS, n_scale=C_HEADS,
                    scale=HEAD_DIM ** -0.5 * math.log2(math.e), name="odd_in")
    y = dilated_attention(z, bsz, seq)
    return matmul_resid([y], w_out3, j, xs, name="odd_out")


def dense_ffn(xs, g, w_gu3, w_d3, j):
    t = xs.shape[0]
    plan = lambda tm: (jnp.full((t // tm,), j, jnp.int32), jnp.full((1,), t // tm, jnp.int32))
    u = rmsnorm(xs, g, BF16, name="ffn_norm")
    h = swiglu_up(u, w_gu3, *plan(DENSE_UP_TILE), tm=DENSE_UP_TILE, name="ffn_up")
    return swiglu_down(h, w_d3, *plan(FFN_TILE), resid=xs, tm=FFN_TILE, name="ffn_down")


def moe_ffn(xs, g, router_w3, w_gu3, w_d3, j):
    t = xs.shape[0]
    meta_i, gates, counts = moe_router(xs, g, router_w3, j)
    pos1, pos2, token_of, tile_expert, n_active = _moe_plan(meta_i, counts, t, FFN_TILE)
    xd = moe_dispatch(xs, g, token_of)
    h = swiglu_up(xd, w_gu3, tile_expert + j * N_EXPERTS, n_active, tm=FFN_TILE, name="moe_up")
    o = swiglu_down(h, w_d3, tile_expert + j * N_EXPERTS, n_active, tm=FFN_TILE, name="moe_down")
    return moe_combine(o, pos1, pos2, gates, xs)


def kernel(x, mem, ln_mix, ln_xattn, ln_mem, ln_ffn, ln_final, ev_w_in, ev_b_gate, ev_conv, ev_sink, ev_head_norm,
           ev_w_out, ffn_w_gu, ffn_w_down, od_w_in, od_w_out, moe_router, moe_w_gu, moe_w_down, x_wq, x_wkv, x_wo):
    bsz, seq, d = x.shape
    mem_len = mem.shape[1]
    depth = ln_mix.shape[0]
    t = bsz * seq
    rope = _rope_tables(seq)
    xs = x.reshape(t, d)
    memf = mem.reshape(bsz * mem_len, d)
    n_odd = od_w_in.shape[0]
    moe_gu = moe_w_gu.reshape(n_odd * N_EXPERTS, d, 2 * FFN_DIM)
    moe_dn = moe_w_down.reshape(n_odd * N_EXPERTS, FFN_DIM, d)
    router_w = jnp.pad(moe_router, ((0, 0), (0, 0), (0, LANES - N_EXPERTS)))
    gate_w = jnp.pad(ev_w_in[:, :, EVEN_MAIN:], ((0, 0), (0, 0), (0, LANES - 4 * B_HEADS)))
    conv_w8 = jnp.pad(ev_conv, ((0, 0), (0, 8 - B_CONV), (0, 0)))

    for layer in range(depth):
        j = layer // 2
        if layer % 2 == 0:
            xs = even_mixer(xs, ln_mix[layer], ev_w_in, gate_w, conv_w8[j], ev_b_gate[j], ev_sink[j],
                            ev_head_norm[j], ev_w_out, j, rope, bsz, seq)
        else:
            xs = odd_mixer(xs, ln_mix[layer], od_w_in, od_w_out, j, rope, bsz, seq)
        kv = norm_matmul(memf, ln_mem[layer], x_wkv, layer, 2 * X_W, name="mem_kv")
        xs = cross_attention(xs, ln_xattn[layer], x_wq, x_wo, layer, kv, seq, mem_len)
        if layer % 2 == 0:
            xs = dense_ffn(xs, ln_ffn[layer], ffn_w_gu, ffn_w_down, j)
        else:
            xs = moe_ffn(xs, ln_ffn[layer], router_w, moe_gu, moe_dn, j)

    out = rmsnorm(xs, ln_final, F32, name="final_norm")
    return out.reshape(bsz, seq, d)
```

```python
import functools
import math

import jax
import jax.numpy as jnp
from jax import lax
from jax.experimental import pallas as pl
from jax.experimental.pallas import tpu as pltpu

D_MODEL = 2048
HEAD_DIM = 128
A_Q_HEADS = 8
A_KV_HEADS = 2
A_HALF_WINDOW = 128
B_HEADS = 4
B_HEAD_DIM = 256
B_CONV = 5
C_HEADS = D_MODEL // HEAD_DIM
C_PATTERNS = ((128, 1), (512, 4), (2048, 16))
X_HEADS = 4
FFN_DIM = 7168
N_EXPERTS = 8
ROPE_THETA = 500000.0
ROT_DIM = HEAD_DIM // 4
EPS = 1e-6

A_Q_W = A_Q_HEADS * HEAD_DIM
A_KV_W = A_KV_HEADS * HEAD_DIM
B_W = B_HEADS * B_HEAD_DIM
X_W = X_HEADS * HEAD_DIM
EVEN_MAIN = A_Q_W + 2 * A_KV_W + 4 * B_W
COL_KA = A_Q_W
COL_VA = A_Q_W + A_KV_W
COL_QKB = A_Q_W + 2 * A_KV_W
COL_VB = COL_QKB + 2 * B_W
COL_OB = COL_VB + B_W

LANES = 128
V7X_VMEM_BYTES = 64 * 1024 * 1024
VMEM_CAP = V7X_VMEM_BYTES - 8 * 1024 * 1024

NEG = -1e30
BF16 = jnp.bfloat16
F32 = jnp.float32


def _params(sem, vmem_bytes):
    return pltpu.CompilerParams(dimension_semantics=sem, vmem_limit_bytes=int(min(VMEM_CAP, vmem_bytes)))


def _nbytes(shape, dtype):
    return math.prod(shape) * jnp.dtype(dtype).itemsize


def _dot(a, b):
    return jnp.dot(a, b, preferred_element_type=F32)


def _dot_nt(a, b):
    return lax.dot_general(a, b, (((1,), (1,)), ((), ())), preferred_element_type=F32)


def _dot_tn(a, b):
    return lax.dot_general(a, b, (((0,), (0,)), ((), ())), preferred_element_type=F32)


def _rms(x, g):
    return x * lax.rsqrt(jnp.mean(x * x, axis=-1, keepdims=True) + EPS) * g


def _rope_tile(z, c, s1, s2):
    return z * c + pltpu.roll(z, LANES - ROT_DIM // 2, 1) * s1 + pltpu.roll(z, ROT_DIM // 2, 1) * s2


def _norm_mm_kernel(x_ref, g_ref, w_ref, c_ref, s1_ref, s2_ref, o_ref, xn_ref, *, tn, n_steps, n_rope, n_scale,
                    scale):
    j = pl.program_id(1)

    @pl.when(j == 0)
    def _():
        xn_ref[...] = _rms(x_ref[...], g_ref[...]).astype(BF16)

    heads = tn // LANES
    half = min(tn, 2 * LANES)

    def step(kinds):
        xn = xn_ref[...]
        for c0 in range(0, tn, half):
            z = _dot(xn, w_ref[:, c0:c0 + half].astype(BF16))
            for hh in range(c0 // LANES, (c0 + half) // LANES):
                zt = z[:, hh * LANES - c0:(hh + 1) * LANES - c0]
                rotate, scaled = kinds[hh]
                if rotate:
                    zt = _rope_tile(zt, c_ref[...], s1_ref[...], s2_ref[...])
                if scaled:
                    zt = zt * scale
                o_ref[:, hh * LANES:(hh + 1) * LANES] = zt.astype(o_ref.dtype)

    kinds_of = lambda jj: tuple((jj * heads + hh < n_rope, jj * heads + hh < n_scale) for hh in range(heads))
    groups = {}
    for jj in range(n_steps):
        groups.setdefault(kinds_of(jj), []).append(jj)
    for kinds, js in groups.items():
        assert js == list(range(js[0], js[-1] + 1))
        if len(groups) == 1:
            step(kinds)
        else:
            pl.when((j >= js[0]) & (j <= js[-1]))(functools.partial(step, kinds))


def norm_matmul(x, g, w3, widx, n_out, *, rope=None, n_rope=0, n_scale=0, scale=1.0, seq=None, tm=1024, tn=512,
                out_dtype=BF16, name="norm_mm"):
    m, k = x.shape
    tm = min(tm, m)
    tn = min(tn, n_out)
    assert m % tm == 0 and n_out % tn == 0 and tn % LANES == 0
    if rope is None:
        dummy = jnp.zeros((8, LANES), F32)
        rope = (dummy, dummy, dummy)
        rspec = pl.BlockSpec((8, LANES), lambda i, j: (0, 0))
    else:
        nsb = seq // tm
        assert seq % tm == 0
        rspec = pl.BlockSpec((tm, LANES), lambda i, j: (i % nsb, 0))
    vm = 2 * _nbytes((tm, k), F32) + _nbytes((tm, k), BF16) + 2 * _nbytes((k, tn), F32) + _nbytes((k, tn), BF16) \
        + 2 * _nbytes((tm, tn), out_dtype) + 2 * _nbytes((tm, tn), F32) + 6 * _nbytes((tm, LANES), F32) + (4 << 20)
    return pl.pallas_call(
        functools.partial(_norm_mm_kernel, tn=tn, n_steps=n_out // tn, n_rope=n_rope, n_scale=n_scale, scale=scale),
        grid=(m // tm, n_out // tn),
        in_specs=[pl.BlockSpec((tm, k), lambda i, j: (i, 0)),
                  pl.BlockSpec((1, k), lambda i, j: (0, 0)),
                  pl.BlockSpec((None, k, tn), lambda i, j: (widx, 0, j)),
                  rspec, rspec, rspec],
        out_specs=pl.BlockSpec((tm, tn), lambda i, j: (i, j)),
        out_shape=jax.ShapeDtypeStruct((m, n_out), out_dtype),
        scratch_shapes=[pltpu.VMEM((tm, k), BF16)],
        compiler_params=_params(("arbitrary", "arbitrary"), vm),
        name=name,
    )(x, g.reshape(1, k), w3, *rope)


def _mm_resid_kernel(*refs, n_x):
    xs = refs[:n_x]
    ws = refs[n_x:2 * n_x]
    r_ref = refs[2 * n_x]
    o_ref = refs[2 * n_x + 1]
    acc = r_ref[...]
    for x_ref, w_ref in zip(xs, ws):
        acc = acc + _dot(x_ref[...].astype(BF16), w_ref[...].astype(BF16))
    o_ref[...] = acc


def matmul_resid(xs, w3, widx, resid, *, tm=512, tn=1024, name="mm_resid"):
    m, kx = xs[0].shape
    n = w3.shape[2]
    n_x = len(xs)
    assert w3.shape[1] == n_x * kx and m % tm == 0 and n % tn == 0
    in_specs = [pl.BlockSpec((tm, kx), lambda j, i: (i, 0)) for _ in xs]
    in_specs += [pl.BlockSpec((None, kx, tn), lambda j, i, q=q: (widx, q, j)) for q in range(n_x)]
    in_specs += [pl.BlockSpec((tm, tn), lambda j, i: (i, j))]
    vm = n_x * (2 * _nbytes((tm, kx), xs[0].dtype) + 3 * _nbytes((kx, tn), F32)) + 6 * _nbytes((tm, tn), F32) + (4 << 20)
    return pl.pallas_call(
        functools.partial(_mm_resid_kernel, n_x=n_x),
        grid=(n // tn, m // tm),
        in_specs=in_specs,
        out_specs=pl.BlockSpec((tm, tn), lambda j, i: (i, j)),
        out_shape=jax.ShapeDtypeStruct((m, n), F32),
        compiler_params=_params(("arbitrary", "arbitrary"), vm),
        name=name,
    )(*xs, *([w3] * n_x), resid)


def _rmsnorm_kernel(x_ref, g_ref, o_ref):
    o_ref[...] = _rms(x_ref[...], g_ref[...]).astype(o_ref.dtype)


def rmsnorm(x, g, out_dtype, *, tm=512, name="rmsnorm"):
    m, k = x.shape
    tm = min(tm, m)
    assert m % tm == 0
    vm = 4 * _nbytes((tm, k), F32) + 2 * _nbytes((tm, k), out_dtype) + (4 << 20)
    return pl.pallas_call(
        _rmsnorm_kernel,
        grid=(m // tm,),
        in_specs=[pl.BlockSpec((tm, k), lambda i: (i, 0)), pl.BlockSpec((1, k), lambda i: (0, 0))],
        out_specs=pl.BlockSpec((tm, k), lambda i: (i, 0)),
        out_shape=jax.ShapeDtypeStruct((m, k), out_dtype),
        compiler_params=_params(("arbitrary",), vm),
        name=name,
    )(x, g.reshape(1, k))


def _win_attn_kernel(sink_ref, q_ref, kp_ref, kc_ref, kn_ref, vp_ref, vc_ref, vn_ref, o_ref, *, tq, seq, grp):
    i = pl.program_id(1)
    kv = pl.program_id(2)
    hw = A_HALF_WINDOW
    k = jnp.concatenate([kp_ref[...], kc_ref[...], kn_ref[...]], axis=0)
    v = jnp.concatenate([vp_ref[...], vc_ref[...], vn_ref[...]], axis=0)
    wk = tq + 2 * hw
    qpos = i * tq + lax.broadcasted_iota(jnp.int32, (tq, wk), 0)
    kpos = i * tq - hw + lax.broadcasted_iota(jnp.int32, (tq, wk), 1)
    valid = (jnp.abs(kpos - qpos) <= hw) & (kpos >= 0) & (kpos < seq)
    for g in range(grp):
        sk = sink_ref[kv * grp + g]
        s = _dot_nt(q_ref[:, g * HEAD_DIM:(g + 1) * HEAD_DIM], k)
        s = jnp.where(valid, s, NEG)
        m = jnp.maximum(jnp.max(s, axis=1, keepdims=True), sk)
        p = jnp.exp(s - m)
        den = jnp.sum(p, axis=1, keepdims=True) + jnp.exp(sk - m)
        o = _dot(p.astype(BF16), v) / den
        o_ref[:, g * HEAD_DIM:(g + 1) * HEAD_DIM] = o.astype(o_ref.dtype)


def window_attention(z, sink, bsz, seq, *, tq=256, name="win_attn"):
    t = z.shape[0]
    hw = A_HALF_WINDOW
    grp = A_Q_HEADS // A_KV_HEADS
    nq = seq // tq
    r = tq // hw
    nhb = seq // hw
    qw = grp * HEAD_DIM
    kcol = COL_KA // HEAD_DIM
    vcol = COL_VA // HEAD_DIM
    cur = lambda col: pl.BlockSpec((tq, HEAD_DIM), lambda b, i, kv, s: (b * nq + i, col + kv))
    prev = lambda col: pl.BlockSpec((hw, HEAD_DIM), lambda b, i, kv, s: (b * nhb + jnp.maximum(i * r - 1, 0), col + kv))
    nxt = lambda col: pl.BlockSpec((hw, HEAD_DIM), lambda b, i, kv, s: (b * nhb + jnp.minimum((i + 1) * r, nhb - 1), col + kv))
    grid_spec = pltpu.PrefetchScalarGridSpec(
        num_scalar_prefetch=1,
        grid=(bsz, nq, A_KV_HEADS),
        in_specs=[pl.BlockSpec((tq, qw), lambda b, i, kv, s: (b * nq + i, kv)),
                  prev(kcol), cur(kcol), nxt(kcol), prev(vcol), cur(vcol), nxt(vcol)],
        out_specs=pl.BlockSpec((tq, qw), lambda b, i, kv, s: (b * nq + i, kv)),
    )
    return pl.pallas_call(
        functools.partial(_win_attn_kernel, tq=tq, seq=seq, grp=grp),
        grid_spec=grid_spec,
        out_shape=jax.ShapeDtypeStruct((t, A_Q_W), BF16),
        compiler_params=_params(("arbitrary",) * 3, 32 << 20),
        name=name,
    )(sink.astype(F32), z, z, z, z, z, z, z)


def _dilated_bias(tq, wk, reach):
    row = lax.broadcasted_iota(jnp.int32, (tq, wk), 0)
    col = lax.broadcasted_iota(jnp.int32, (tq, wk), 1)
    d = col - reach - row
    ad = jnp.abs(d)
    mult = jnp.zeros((tq, wk), jnp.int32)
    for window, dil in C_PATTERNS:
        mult = mult + ((ad <= window // 2) & ((d & (dil - 1)) == 0)).astype(jnp.int32)
    bias = jnp.where(mult == 1, 0.0, jnp.where(mult == 2, 1.0, math.log2(3.0)))
    return jnp.where(mult == 0, NEG, bias).astype(F32)


def _dil_attn_kernel(q_ref, k_ref, v_ref, o_ref, kp_ref, vp_ref, bias_ref, *, tq, seq, reach, nh, rb):
    b = pl.program_id(0)
    hg = pl.program_id(1)
    i = pl.program_id(2)
    wk = tq + 2 * reach
    hd = HEAD_DIM

    @pl.when((b == 0) & (hg == 0) & (i == 0))
    def _():
        bias_ref[...] = _dilated_bias(tq, wk, reach)
        kp_ref[...] = jnp.zeros_like(kp_ref)
        vp_ref[...] = jnp.zeros_like(vp_ref)
        lane = lax.broadcasted_iota(jnp.int32, (seq, hd), 1)
        ones_col = jnp.where(lane == 0, 1.0, 0.0).astype(BF16)
        for h in range(nh):
            vp_ref[h, reach:reach + seq, hd:2 * hd] = ones_col

    @pl.when(i == 0)
    def _():
        for h in range(nh):
            kp_ref[h, reach:reach + seq, :] = k_ref[:, h * hd:(h + 1) * hd]
            vp_ref[h, reach:reach + seq, 0:hd] = v_ref[:, h * hd:(h + 1) * hd]

    start = pl.multiple_of(i * tq, tq)

    def attend(edge):
        if edge:
            kpos = i * tq - reach + lax.broadcasted_iota(jnp.int32, (1, wk), 1)
            colbias = jnp.where((kpos >= 0) & (kpos < seq), 0.0, NEG).astype(F32)
        for h in range(nh):
            ks = kp_ref[h, pl.ds(start, wk), :]
            vs = vp_ref[h, pl.ds(start, wk), :]
            for r0 in range(0, tq, rb):
                s = _dot_nt(q_ref[r0:r0 + rb, h * hd:(h + 1) * hd], ks) + bias_ref[r0:r0 + rb, :]
                if edge:
                    s = s + colbias
                m = jnp.max(s, axis=1, keepdims=True)
                p = jnp.exp2(s - m).astype(BF16)
                pv = _dot(p, vs)
                o_ref[r0:r0 + rb, h * hd:(h + 1) * hd] = (pv[:, 0:hd] / pv[:, hd:hd + 1]).astype(o_ref.dtype)

    is_edge = (i * tq < reach) | (i * tq + tq + reach > seq)
    pl.when(is_edge)(lambda: attend(True))
    pl.when(jnp.logical_not(is_edge))(lambda: attend(False))


def dilated_attention(z, bsz, seq, *, tq=256, nh=2, rb=128, name="dil_attn"):
    t = z.shape[0]
    reach = max(w // 2 for w, _ in C_PATTERNS)
    nq = seq // tq
    wk = tq + 2 * reach
    ng = C_HEADS // nh
    gw = nh * HEAD_DIM
    assert seq % tq == 0 and seq >= wk and C_HEADS % nh == 0
    vm = 4 * _nbytes((seq, gw), BF16) + 3 * nh * _nbytes((seq + 2 * reach, HEAD_DIM), BF16) \
        + (1 + 3 * nh) * _nbytes((tq, wk), F32) + (8 << 20)
    return pl.pallas_call(
        functools.partial(_dil_attn_kernel, tq=tq, seq=seq, reach=reach, nh=nh, rb=rb),
        grid=(bsz, ng, nq),
        in_specs=[pl.BlockSpec((tq, gw), lambda b, g, i: (b * nq + i, g)),
                  pl.BlockSpec((seq, gw), lambda b, g, i: (b, ng + g)),
                  pl.BlockSpec((seq, gw), lambda b, g, i: (b, 2 * ng + g))],
        out_specs=pl.BlockSpec((tq, gw), lambda b, g, i: (b * nq + i, g)),
        out_shape=jax.ShapeDtypeStruct((t, D_MODEL), BF16),
        scratch_shapes=[pltpu.VMEM((nh, seq + 2 * reach, HEAD_DIM), BF16),
                        pltpu.VMEM((nh, seq + 2 * reach, 2 * HEAD_DIM), BF16),
                        pltpu.VMEM((tq, wk), F32)],
        compiler_params=_params(("arbitrary",) * 3, vm),
        name=name,
    )(z, z, z)


def _xattn_kernel(x_ref, g_ref, wq_ref, kv_ref, wo_ref, gn_ref, o_ref, *rest, emit_next):
    wq_s, wo_s = rest[-2:]

    @pl.when(pl.program_id(0) == 0)
    def _():
        wq_s[...] = wq_ref[...].astype(BF16)
        wo_s[...] = wo_ref[...].astype(BF16)

    x = x_ref[...]
    xn = _rms(x, g_ref[...]).astype(BF16)
    q = (_dot(xn, wq_s[...]) * (HEAD_DIM ** -0.5)).astype(BF16)
    kv = kv_ref[...]
    outs = []
    for h in range(X_HEADS):
        k = kv[:, h * HEAD_DIM:(h + 1) * HEAD_DIM]
        v = kv[:, X_W + h * HEAD_DIM:X_W + (h + 1) * HEAD_DIM]
        s = _dot_nt(q[:, h * HEAD_DIM:(h + 1) * HEAD_DIM], k)
        m = jnp.max(s, axis=1, keepdims=True)
        p = jnp.exp(s - m)
        den = jnp.sum(p, axis=1, keepdims=True)
        outs.append((_dot(p.astype(BF16), v) / den).astype(BF16))
    o = jnp.concatenate(outs, axis=1)
    y = x + _dot(o, wo_s[...])
    o_ref[...] = y
    if emit_next:
        rest[0][...] = _rms(y, gn_ref[...]).astype(BF16)


def cross_attention(x, g, wq3, wo3, layer, kv, seq, mem_len, next_gain=None, *, tm=512, name="xattn"):
    t, d = x.shape
    nsb = seq // tm
    assert seq % tm == 0
    emit_next = next_gain is not None
    vm = 4 * _nbytes((tm, d), F32) + 3 * _nbytes((d, X_W), F32) * 2 + 8 * _nbytes((tm, d), F32) + (8 << 20)
    row = pl.BlockSpec((tm, d), lambda i: (i, 0))
    vec = pl.BlockSpec((1, d), lambda i: (0, 0))
    out = pl.pallas_call(
        functools.partial(_xattn_kernel, emit_next=emit_next),
        grid=(t // tm,),
        in_specs=[row, vec,
                  pl.BlockSpec((None, d, X_W), lambda i: (layer, 0, 0)),
                  pl.BlockSpec((mem_len, 2 * X_W), lambda i: (i // nsb, 0)),
                  pl.BlockSpec((None, X_W, d), lambda i: (layer, 0, 0)),
                  vec],
        out_specs=[row, row] if emit_next else [row],
        out_shape=[jax.ShapeDtypeStruct((t, d), F32)] + ([jax.ShapeDtypeStruct((t, d), BF16)] if emit_next else []),
        scratch_shapes=[pltpu.VMEM((d, X_W), BF16), pltpu.VMEM((X_W, d), BF16)],
        compiler_params=_params(("arbitrary",), vm),
        name=name,
    )(x, g.reshape(1, d), wq3, kv, wo3, (next_gain if emit_next else g).reshape(1, d))
    return tuple(out) if emit_next else out[0]


def _conv_kernel(xp_ref, xc_ref, xn_ref, w_ref, o_ref, *, ts, ns, halo, q_scale, n_q_blocks):
    i = pl.program_id(1)
    j = pl.program_id(2)
    pad = B_CONV // 2
    xp = jnp.where(i > 0, xp_ref[halo - pad:, :].astype(F32), 0.0)
    xn = jnp.where(i < ns - 1, xn_ref[:pad, :].astype(F32), 0.0)
    xx = jnp.concatenate([xp, xc_ref[...].astype(F32), xn], axis=0)
    w = w_ref[...]
    acc = xx[0:ts, :] * w[0:1, :]
    for tap in range(1, B_CONV):
        acc = acc + xx[tap:tap + ts, :] * w[tap:tap + 1, :]
    y = acc * jax.nn.sigmoid(acc)
    y = y * jnp.where(j < n_q_blocks, q_scale, 1.0)
    o_ref[...] = y.astype(o_ref.dtype)


def conv_silu(z, conv_w8, bsz, seq, *, ts=512, tc=512, name="conv_silu"):
    t = z.shape[0]
    halo = 16
    ns = seq // ts
    nh = seq // halo
    cb = COL_QKB // tc
    assert COL_QKB % tc == 0 and seq % ts == 0 and B_W % tc == 0
    return pl.pallas_call(
        functools.partial(_conv_kernel, ts=ts, ns=ns, halo=halo, q_scale=B_HEAD_DIM ** -0.5, n_q_blocks=B_W // tc),
        grid=(bsz, ns, 2 * B_W // tc),
        in_specs=[pl.BlockSpec((halo, tc), lambda b, i, j: (b * nh + jnp.maximum(i * (ts // halo) - 1, 0), cb + j)),
                  pl.BlockSpec((ts, tc), lambda b, i, j: (b * ns + i, cb + j)),
                  pl.BlockSpec((halo, tc), lambda b, i, j: (b * nh + jnp.minimum((i + 1) * (ts // halo), nh - 1), cb + j)),
                  pl.BlockSpec((8, tc), lambda b, i, j: (0, j))],
        out_specs=pl.BlockSpec((ts, tc), lambda b, i, j: (b * ns + i, j)),
        out_shape=jax.ShapeDtypeStruct((t, 2 * B_W), BF16),
        compiler_params=_params(("arbitrary",) * 3, 32 << 20),
        name=name,
    )(z, z, z, conv_w8)


def _log_sigmoid(x):
    return jnp.minimum(x, 0.0) - jnp.log(1.0 + jnp.exp(-jnp.abs(x)))


def _mlstm_chunk(q, k, v, i_col, f_col, i_row, f_row, c_ref, n_ref, m_ref, rev):
    L = q.shape[0]
    logf_c = _log_sigmoid(f_col)
    logf_r = _log_sigmoid(f_row)
    row = lax.broadcasted_iota(jnp.int32, (L, L), 0)
    col = lax.broadcasted_iota(jnp.int32, (L, L), 1)
    causal = (col >= row) if rev else (col <= row)
    causal_t = (row >= col) if rev else (row <= col)
    b_col = jnp.sum(jnp.where(causal, logf_r, 0.0), axis=1, keepdims=True)
    b_row = jnp.sum(jnp.where(causal_t, logf_c, 0.0), axis=0, keepdims=True)
    b_all = jnp.sum(logf_r, axis=1, keepdims=True)
    m_st = m_ref[...]
    c_st = c_ref[...]
    n_st = n_ref[...]
    logd = jnp.where(causal, b_col - b_row + i_row, NEG)
    m_inter = b_col + m_st
    mt = jnp.maximum(m_inter, jnp.max(logd, axis=1, keepdims=True))
    s = _dot_nt(q, k) * jnp.exp(logd - mt)
    sc = jnp.exp(m_inter - mt)
    num = _dot(s.astype(BF16), v) + sc * _dot(q, c_st.astype(BF16))
    den = jnp.sum(s, axis=1, keepdims=True) + sc * jnp.sum(q.astype(F32) * n_st, axis=1, keepdims=True)
    h = num / jnp.maximum(jnp.abs(den), jnp.exp(-mt))
    logw = b_all - b_col + i_col
    m_new = jnp.maximum(b_all + m_st, jnp.max(logw, axis=0, keepdims=True))
    wgt = jnp.exp(logw - m_new)
    dec = jnp.exp(b_all + m_st - m_new)
    kw = k.astype(F32) * wgt
    c_ref[...] = dec * c_st + _dot_tn(kw.astype(BF16), v)
    n_ref[...] = dec * n_st + jnp.sum(kw, axis=0, keepdims=True)
    m_ref[...] = m_new
    return h


def _mlstm_kernel(bias_ref, qf_ref, kf_ref, vf0_ref, vf1_ref, gcf_ref, grf_ref,
                  qb_ref, kb_ref, vb0_ref, vb1_ref, gcb_ref, grb_ref,
                  hf_ref, hb_ref, c_ref, n_ref, m_ref):
    @pl.when(pl.program_id(1) == 0)
    def _():
        c_ref[...] = jnp.zeros_like(c_ref)
        n_ref[...] = jnp.zeros_like(n_ref)
        m_ref[...] = jnp.zeros_like(m_ref)

    lane = lax.broadcasted_iota(jnp.int32, (1, LANES), 1)
    dh = B_HEAD_DIM
    for d, (q_ref, k_ref, v0_ref, v1_ref, gc_ref, gr_ref, h_ref) in enumerate((
            (qf_ref, kf_ref, vf0_ref, vf1_ref, gcf_ref, grf_ref, hf_ref),
            (qb_ref, kb_ref, vb0_ref, vb1_ref, gcb_ref, grb_ref, hb_ref))):
        gc = gc_ref[...]
        for hd in range(B_HEADS):
            ci = 2 * d * B_HEADS + hd
            cf = ci + B_HEADS
            i_col = jnp.sum(jnp.where(lane == ci, gc, 0.0), axis=1, keepdims=True) + bias_ref[ci]
            f_col = jnp.sum(jnp.where(lane == cf, gc, 0.0), axis=1, keepdims=True) + bias_ref[cf]
            i_row = gr_ref[ci:ci + 1, :] + bias_ref[ci]
            f_row = gr_ref[cf:cf + 1, :] + bias_ref[cf]
            v_ref = v0_ref if hd < B_HEADS // 2 else v1_ref
            vo = (hd % (B_HEADS // 2)) * dh
            st = d * B_HEADS + hd
            h = _mlstm_chunk(q_ref[:, hd * dh:(hd + 1) * dh], k_ref[:, hd * dh:(hd + 1) * dh],
                             v_ref[:, vo:vo + dh], i_col, f_col, i_row, f_row,
                             c_ref.at[st], n_ref.at[st], m_ref.at[st], rev=bool(d))
            h_ref[:, hd * dh:(hd + 1) * dh] = h


def mlstm(qk, z, g_cols, g_rows, b_gate, bsz, seq, *, chunk=256, name="mlstm"):
    t = qk.shape[0]
    nc = seq // chunk
    assert seq % chunk == 0
    half = B_W // 2
    vcol = COL_VB // half
    assert COL_VB % half == 0
    fw = lambda b, c, s: b * nc + c
    bw = lambda b, c, s: b * nc + (nc - 1 - c)

    def specs(rowf):
        return [pl.BlockSpec((chunk, B_W), lambda b, c, s: (rowf(b, c, s), 0)),
                pl.BlockSpec((chunk, B_W), lambda b, c, s: (rowf(b, c, s), 1)),
                pl.BlockSpec((chunk, half), lambda b, c, s: (rowf(b, c, s), vcol)),
                pl.BlockSpec((chunk, half), lambda b, c, s: (rowf(b, c, s), vcol + 1)),
                pl.BlockSpec((chunk, LANES), lambda b, c, s: (rowf(b, c, s), 0)),
                pl.BlockSpec((16, chunk), lambda b, c, s: (0, rowf(b, c, s)))]

    grid_spec = pltpu.PrefetchScalarGridSpec(
        num_scalar_prefetch=1,
        grid=(bsz, nc),
        in_specs=specs(fw) + specs(bw),
        out_specs=[pl.BlockSpec((chunk, B_W), lambda b, c, s: (fw(b, c, s), 0)),
                   pl.BlockSpec((chunk, B_W), lambda b, c, s: (bw(b, c, s), 0))],
        scratch_shapes=[pltpu.VMEM((2 * B_HEADS, B_HEAD_DIM, B_HEAD_DIM), F32),
                        pltpu.VMEM((2 * B_HEADS, 1, B_HEAD_DIM), F32),
                        pltpu.VMEM((2 * B_HEADS, 1, 1), F32)],
    )
    args = (qk, qk, z, z, g_cols, g_rows)
    return pl.pallas_call(
        _mlstm_kernel,
        grid_spec=grid_spec,
        out_shape=[jax.ShapeDtypeStruct((t, B_W), F32), jax.ShapeDtypeStruct((t, B_W), F32)],
        compiler_params=_params(("arbitrary", "arbitrary"), 48 << 20),
        name=name,
    )(b_gate.astype(F32), *args, *args)


def _mlstm_out_kernel(hf_ref, hb_ref, o0_ref, o1_ref, gain_ref, y_ref):
    dh = B_HEAD_DIM
    for hd in range(B_HEADS):
        sl = slice(hd * dh, (hd + 1) * dh)
        h = hf_ref[:, sl] + hb_ref[:, sl]
        hc = h - jnp.mean(h, axis=1, keepdims=True)
        y = hc * lax.rsqrt(jnp.mean(hc * hc, axis=1, keepdims=True) + EPS) * gain_ref[:, sl]
        o_ref = o0_ref if hd < B_HEADS // 2 else o1_ref
        oo = (hd % (B_HEADS // 2)) * dh
        y_ref[:, sl] = (jax.nn.sigmoid(o_ref[:, oo:oo + dh].astype(F32)) * y).astype(y_ref.dtype)


def mlstm_output(hf, hb, z, gain, *, tm=512, name="mlstm_out"):
    t = hf.shape[0]
    half = B_W // 2
    ocol = COL_OB // half
    assert COL_OB % half == 0 and t % tm == 0
    return pl.pallas_call(
        _mlstm_out_kernel,
        grid=(t // tm,),
        in_specs=[pl.BlockSpec((tm, B_W), lambda i: (i, 0)),
                  pl.BlockSpec((tm, B_W), lambda i: (i, 0)),
                  pl.BlockSpec((tm, half), lambda i: (i, ocol)),
                  pl.BlockSpec((tm, half), lambda i: (i, ocol + 1)),
                  pl.BlockSpec((1, B_W), lambda i: (0, 0))],
        out_specs=pl.BlockSpec((tm, B_W), lambda i: (i, 0)),
        out_shape=jax.ShapeDtypeStruct((t, B_W), BF16),
        compiler_params=_params(("arbitrary",), 32 << 20),
        name=name,
    )(hf, hb, z, z, gain.reshape(1, B_W))


def _up_kernel(te_ref, na_ref, x_ref, wg_ref, wu_ref, h_ref):
    @pl.when(pl.program_id(1) < na_ref[0])
    def _():
        x = x_ref[...]
        g = _dot(x, wg_ref[...].astype(BF16))
        u = _dot(x, wu_ref[...].astype(BF16))
        h_ref[...] = (g * jax.nn.sigmoid(g) * u).astype(h_ref.dtype)

    @pl.when(pl.program_id(1) >= na_ref[0])
    def _():
        h_ref[...] = jnp.zeros_like(h_ref)


def swiglu_up(x, w_gu3, tile_expert, n_active, *, tm=512, tf=512, name="ffn_up"):
    p, d = x.shape
    f = w_gu3.shape[2] // 2
    nt = p // tm
    nj = f // tf
    assert p % tm == 0 and f % tf == 0
    row = lambda i, na: jnp.minimum(i, na[0] - 1)
    grid_spec = pltpu.PrefetchScalarGridSpec(
        num_scalar_prefetch=2,
        grid=(nj, nt),
        in_specs=[pl.BlockSpec((tm, d), lambda j, i, te, na: (row(i, na), 0)),
                  pl.BlockSpec((None, d, tf), lambda j, i, te, na: (te[row(i, na)], 0, j)),
                  pl.BlockSpec((None, d, tf), lambda j, i, te, na: (te[row(i, na)], 0, nj + j))],
        out_specs=pl.BlockSpec((tm, tf), lambda j, i, te, na: (i, j)),
    )
    vm = 2 * _nbytes((tm, d), BF16) + 2 * 3 * _nbytes((d, tf), F32) + 8 * _nbytes((tm, tf), F32) + (4 << 20)
    return pl.pallas_call(
        _up_kernel,
        grid_spec=grid_spec,
        out_shape=jax.ShapeDtypeStruct((p, f), BF16),
        compiler_params=_params(("arbitrary", "arbitrary"), vm),
        name=name,
    )(tile_expert, n_active, x, w_gu3, w_gu3)


def _down_kernel(te_ref, na_ref, h_ref, w_ref, *rest, has_resid):
    o_ref = rest[-1]

    @pl.when(pl.program_id(1) < na_ref[0])
    def _():
        o = _dot(h_ref[...], w_ref[...].astype(BF16))
        if has_resid:
            o = o + rest[0][...]
        o_ref[...] = o

    @pl.when(pl.program_id(1) >= na_ref[0])
    def _():
        o_ref[...] = jnp.zeros_like(o_ref)


def swiglu_down(h, w_d3, tile_expert, n_active, resid=None, *, tm=512, tn=512, name="ffn_down"):
    p, f = h.shape
    d = w_d3.shape[2]
    nt = p // tm
    assert p % tm == 0 and d % tn == 0
    row = lambda i, na: jnp.minimum(i, na[0] - 1)
    in_specs = [pl.BlockSpec((tm, f), lambda n, i, te, na: (row(i, na), 0)),
                pl.BlockSpec((None, f, tn), lambda n, i, te, na: (te[row(i, na)], 0, n))]
    args = [h, w_d3]
    if resid is not None:
        in_specs.append(pl.BlockSpec((tm, tn), lambda n, i, te, na: (row(i, na), n)))
        args.append(resid)
    grid_spec = pltpu.PrefetchScalarGridSpec(
        num_scalar_prefetch=2,
        grid=(d // tn, nt),
        in_specs=in_specs,
        out_specs=pl.BlockSpec((tm, tn), lambda n, i, te, na: (i, n)),
    )
    vm = 2 * _nbytes((tm, f), BF16) + 2 * _nbytes((f, tn), F32) + _nbytes((f, tn), BF16) + 8 * _nbytes((tm, tn), F32) + (4 << 20)
    return pl.pallas_call(
        functools.partial(_down_kernel, has_resid=resid is not None),
        grid_spec=grid_spec,
        out_shape=jax.ShapeDtypeStruct((p, d), F32),
        compiler_params=_params(("arbitrary", "arbitrary"), vm),
        name=name,
    )(tile_expert, n_active, *args)


def _router_kernel(x_ref, g_ref, w_ref, mi_ref, mf_ref, cnt_ref, carry_ref, *, tm):
    @pl.when(pl.program_id(0) == 0)
    def _():
        carry_ref[...] = jnp.zeros_like(carry_ref)

    u = _rms(x_ref[...], g_ref[...])
    logits = jnp.dot(u, w_ref[...], preferred_element_type=F32, precision=lax.Precision.HIGHEST)
    lane = lax.broadcasted_iota(jnp.int32, (tm, LANES), 1)
    logits = jnp.where(lane < N_EXPERTS, logits, -jnp.inf)
    v1 = jnp.max(logits, axis=1, keepdims=True)
    i1 = jnp.min(jnp.where(logits == v1, lane, LANES), axis=1, keepdims=True)
    oh1 = lane == i1
    rest = jnp.where(oh1, -jnp.inf, logits)
    v2 = jnp.max(rest, axis=1, keepdims=True)
    i2 = jnp.min(jnp.where(rest == v2, lane, LANES), axis=1, keepdims=True)
    oh2 = lane == i2
    e = jnp.exp(v2 - v1)
    g1 = 1.0 / (1.0 + e)
    g2 = e * g1
    cnt = (oh1 | oh2).astype(F32)
    r = lax.broadcasted_iota(jnp.int32, (tm, tm), 0)
    c = lax.broadcasted_iota(jnp.int32, (tm, tm), 1)
    before = (c < r).astype(BF16)
    excl = _dot(before, cnt.astype(BF16)) + carry_ref[0:1, :]
    r1 = jnp.sum(jnp.where(oh1, excl, 0.0), axis=1, keepdims=True).astype(jnp.int32)
    r2 = jnp.sum(jnp.where(oh2, excl, 0.0), axis=1, keepdims=True).astype(jnp.int32)
    mi_ref[...] = jnp.where(lane == 0, i1, jnp.where(lane == 1, i2, jnp.where(lane == 2, r1, jnp.where(lane == 3, r2, 0))))
    mf_ref[...] = jnp.where(lane == 0, g1, jnp.where(lane == 1, g2, 0.0))
    carry_ref[...] = carry_ref[...] + jnp.sum(cnt, axis=0, keepdims=True)
    cnt_ref[...] = carry_ref[...]


def moe_router(x, g, w_router3, layer, *, tm=512, name="moe_router"):
    t, d = x.shape
    assert t % tm == 0
    vm = 6 * _nbytes((tm, d), F32) + 4 * _nbytes((d, LANES), F32) + 8 * _nbytes((tm, tm), F32) + (4 << 20)
    return pl.pallas_call(
        functools.partial(_router_kernel, tm=tm),
        grid=(t // tm,),
        in_specs=[pl.BlockSpec((tm, d), lambda i: (i, 0)),
                  pl.BlockSpec((1, d), lambda i: (0, 0)),
                  pl.BlockSpec((None, d, LANES), lambda i: (layer, 0, 0))],
        out_specs=[pl.BlockSpec((tm, LANES), lambda i: (i, 0)),
                   pl.BlockSpec((tm, LANES), lambda i: (i, 0)),
                   pl.BlockSpec((8, LANES), lambda i: (0, 0))],
        out_shape=[jax.ShapeDtypeStruct((t, LANES), jnp.int32),
                   jax.ShapeDtypeStruct((t, LANES), F32),
                   jax.ShapeDtypeStruct((8, LANES), F32)],
        scratch_shapes=[pltpu.VMEM((8, LANES), F32)],
        compiler_params=_params(("arbitrary",), vm),
        name=name,
    )(x, g.reshape(1, d), w_router3)


def _row_copy(src_hbm, dst_vmem, sem, src_row, dst_row):
    return pltpu.make_async_copy(src_hbm.at[pl.ds(src_row, 1), :], dst_vmem.at[pl.ds(dst_row, 1), :], sem)


GATHER_UNROLL = 8


def _start_rows(src_hbm, idx_refs, dst_refs, sem, n):
    def body(g, carry):
        for u in range(GATHER_UNROLL):
            r = g * GATHER_UNROLL + u
            for idx_ref, dst in zip(idx_refs, dst_refs):
                _row_copy(src_hbm, dst, sem, idx_ref[0, 0, r], r).start(priority=u % 2)
        return carry

    assert n % GATHER_UNROLL == 0
    lax.fori_loop(0, n // GATHER_UNROLL, body, 0)


def _wait_rows(src_hbm, dst_refs, sem, n):
    def body(r, carry):
        for dst in dst_refs:
            _row_copy(src_hbm, dst, sem, 0, r).wait()
        return carry

    lax.fori_loop(0, n, body, 0, unroll=GATHER_UNROLL)


def _dispatch_kernel(tok_ref, tok_next_ref, x_hbm, g_ref, o_ref, rows_ref, sems, *, tg):
    i = pl.program_id(0)
    slot = i % 2

    @pl.when(i == 0)
    def _():
        _start_rows(x_hbm, [tok_ref], [rows_ref.at[0]], sems.at[0], tg)

    @pl.when(i + 1 < pl.num_programs(0))
    def _():
        _start_rows(x_hbm, [tok_next_ref], [rows_ref.at[1 - slot]], sems.at[1 - slot], tg)

    _wait_rows(x_hbm, [rows_ref.at[slot]], sems.at[slot], tg)
    o_ref[...] = _rms(rows_ref[slot], g_ref[...]).astype(o_ref.dtype)


def moe_dispatch(x, g, token_of, *, tg=256, name="moe_dispatch"):
    t, d = x.shape
    p = token_of.shape[0]
    assert p % tg == 0
    nt = p // tg
    vm = 4 * _nbytes((tg, d), F32) + 2 * _nbytes((tg, d), BF16) + (4 << 20)
    tok3 = token_of.reshape(nt, 1, tg)
    return pl.pallas_call(
        functools.partial(_dispatch_kernel, tg=tg),
        grid=(nt,),
        in_specs=[pl.BlockSpec((1, 1, tg), lambda i: (i, 0, 0), memory_space=pltpu.SMEM),
                  pl.BlockSpec((1, 1, tg), lambda i: (jnp.minimum(i + 1, nt - 1), 0, 0), memory_space=pltpu.SMEM),
                  pl.BlockSpec(memory_space=pl.ANY),
                  pl.BlockSpec((1, d), lambda i: (0, 0))],
        out_specs=pl.BlockSpec((tg, d), lambda i: (i, 0)),
        out_shape=jax.ShapeDtypeStruct((p, d), BF16),
        scratch_shapes=[pltpu.VMEM((2, tg, d), F32), pltpu.SemaphoreType.DMA((2,))],
        compiler_params=_params(("arbitrary",), vm),
        name=name,
    )(tok3, tok3, x, g.reshape(1, d))


def _combine_kernel(p1_ref, p2_ref, p1n_ref, p2n_ref, o_hbm, mf_ref, x_ref, gf_ref, y_ref, rows_ref, sems, *, tc,
                    final_norm):
    i = pl.program_id(0)
    slot = i % 2
    dst = lambda s: [rows_ref.at[s, 0], rows_ref.at[s, 1]]

    @pl.when(i == 0)
    def _():
        _start_rows(o_hbm, [p1_ref, p2_ref], dst(0), sems.at[0], tc)

    @pl.when(i + 1 < pl.num_programs(0))
    def _():
        _start_rows(o_hbm, [p1n_ref, p2n_ref], dst(1 - slot), sems.at[1 - slot], tc)

    _wait_rows(o_hbm, dst(slot), sems.at[slot], tc)
    mf = mf_ref[...]
    y = x_ref[...] + mf[:, 0:1] * rows_ref[slot, 0] + mf[:, 1:2] * rows_ref[slot, 1]
    y_ref[...] = _rms(y, gf_ref[...]) if final_norm else y


def moe_combine(o, pos1, pos2, gates, x, final_gain=None, *, tc=256, name="moe_combine"):
    t, d = x.shape
    final_norm = final_gain is not None
    assert t % tc == 0
    nt = t // tc
    vm = 4 * _nbytes((tc, d), F32) + 6 * _nbytes((tc, d), F32) + (4 << 20)
    cur = pl.BlockSpec((1, 1, tc), lambda i: (i, 0, 0), memory_space=pltpu.SMEM)
    nxt = pl.BlockSpec((1, 1, tc), lambda i: (jnp.minimum(i + 1, nt - 1), 0, 0), memory_space=pltpu.SMEM)
    p1 = pos1.reshape(nt, 1, tc)
    p2 = pos2.reshape(nt, 1, tc)
    return pl.pallas_call(
        functools.partial(_combine_kernel, tc=tc, final_norm=final_norm),
        grid=(nt,),
        in_specs=[cur, cur, nxt, nxt,
                  pl.BlockSpec(memory_space=pl.ANY),
                  pl.BlockSpec((tc, LANES), lambda i: (i, 0)),
                  pl.BlockSpec((tc, d), lambda i: (i, 0)),
                  pl.BlockSpec((1, d), lambda i: (0, 0))],
        out_specs=pl.BlockSpec((tc, d), lambda i: (i, 0)),
        out_shape=jax.ShapeDtypeStruct((t, d), F32),
        scratch_shapes=[pltpu.VMEM((2, 2, tc, d), F32), pltpu.SemaphoreType.DMA((2,))],
        compiler_params=_params(("arbitrary",), vm),
        name=name,
    )(p1, p2, p1, p2, o, gates, x, (final_gain if final_norm else jnp.ones((d,), F32)).reshape(1, d))


def _rope_tables(seq):
    pos = jnp.arange(seq, dtype=F32)
    inv = jnp.power(ROPE_THETA, -jnp.arange(0, ROT_DIM, 2, dtype=F32) / ROT_DIM)
    ang = pos[:, None] * inv[None, :]
    cos, sin = jnp.cos(ang), jnp.sin(ang)
    half = ROT_DIM // 2
    zeros = jnp.zeros((seq, LANES - ROT_DIM), F32)
    c = jnp.concatenate([cos, cos, jnp.ones_like(zeros)], axis=1)
    s1 = jnp.concatenate([-sin, jnp.zeros((seq, half), F32), zeros], axis=1)
    s2 = jnp.concatenate([jnp.zeros((seq, half), F32), sin, zeros], axis=1)
    return c, s1, s2


def _moe_plan(meta_i, counts, t, tm):
    eid = meta_i[:, 0:2]
    rank = meta_i[:, 2:4]
    cnt = counts[0, :N_EXPERTS].astype(jnp.int32)
    padded = ((cnt + tm - 1) // tm) * tm
    ends = jnp.cumsum(padded)
    starts = ends - padded
    pos = starts[eid] + rank
    p_max = 2 * t + N_EXPERTS * tm
    nt = p_max // tm
    n_active = jnp.maximum(ends[-1] // tm, 1)
    tile_start = jnp.minimum(jnp.arange(nt, dtype=jnp.int32), n_active - 1) * tm
    tile_expert = jnp.minimum(jnp.searchsorted(ends, tile_start, side="right"), N_EXPERTS - 1).astype(jnp.int32)
    tok = jnp.broadcast_to(jnp.arange(t, dtype=jnp.int32)[:, None], (t, 2))
    token_of = jnp.zeros((p_max,), jnp.int32).at[pos.reshape(-1)].set(tok.reshape(-1))
    return pos[:, 0], pos[:, 1], token_of, tile_expert, n_active.reshape(1).astype(jnp.int32)


FFN_TILE = 512
MOE_UP_COLS = 1024
DENSE_UP_TILE = 1024


def even_mixer(xs, g, w_in3, gate_w3, conv_w8, b_gate, sink, head_gain, w_out3, j, rope, bsz, seq):
    z = norm_matmul(xs, g, w_in3, j, EVEN_MAIN, rope=rope, seq=seq, n_rope=(A_Q_W + A_KV_W) // HEAD_DIM,
                    n_scale=A_Q_W // HEAD_DIM, scale=HEAD_DIM ** -0.5, name="even_in")
    g_cols = norm_matmul(xs, g, gate_w3, j, LANES, out_dtype=F32, name="even_gates")
    g_rows = g_cols[:, :4 * B_HEADS].T
    ya = window_attention(z, sink, bsz, seq)
    qk = conv_silu(z, conv_w8, bsz, seq)
    hf, hb = mlstm(qk, z, g_cols, g_rows, b_gate, bsz, seq)
    yb = mlstm_output(hf, hb, z, head_gain)
    return matmul_resid([ya, yb], w_out3, j, xs, name="even_out")


def odd_mixer(xs, g, w_in3, w_out3, j, rope, bsz, seq):
    d = xs.shape[1]
    z = norm_matmul(xs, g, w_in3, j, 3 * d, rope=rope, seq=seq, n_rope=2 * C_HEADS, n_scale=C_HEADS,
                    scale=HEAD_DIM ** -0.5 * math.log2(math.e), name="odd_in")
    y = dilated_attention(z, bsz, seq)
    return matmul_resid([y], w_out3, j, xs, name="odd_out")


def dense_ffn(xs, u, w_gu3, w_d3, j):
    t = xs.shape[0]
    plan = lambda tm: (jnp.full((t // tm,), j, jnp.int32), jnp.full((1,), t // tm, jnp.int32))
    h = swiglu_up(u, w_gu3, *plan(DENSE_UP_TILE), tm=DENSE_UP_TILE, name="ffn_up")
    return swiglu_down(h, w_d3, *plan(FFN_TILE), resid=xs, tm=FFN_TILE, name="ffn_down")


def moe_ffn(xs, g, router_w3, w_gu3, w_d3, j, final_gain=None):
    t = xs.shape[0]
    meta_i, gates, counts = moe_router(xs, g, router_w3, j)
    pos1, pos2, token_of, tile_expert, n_active = _moe_plan(meta_i, counts, t, FFN_TILE)
    xd = moe_dispatch(xs, g, token_of)
    h = swiglu_up(xd, w_gu3, tile_expert + j * N_EXPERTS, n_active, tm=FFN_TILE, tf=MOE_UP_COLS, name="moe_up")
    o = swiglu_down(h, w_d3, tile_expert + j * N_EXPERTS, n_active, tm=FFN_TILE, name="moe_down")
    return moe_combine(o, pos1, pos2, gates, xs, final_gain)


def kernel(x, mem, ln_mix, ln_xattn, ln_mem, ln_ffn, ln_final, ev_w_in, ev_b_gate, ev_conv, ev_sink, ev_head_norm,
           ev_w_out, ffn_w_gu, ffn_w_down, od_w_in, od_w_out, moe_router, moe_w_gu, moe_w_down, x_wq, x_wkv, x_wo):
    bsz, seq, d = x.shape
    mem_len = mem.shape[1]
    depth = ln_mix.shape[0]
    t = bsz * seq
    rope = _rope_tables(seq)
    xs = x.reshape(t, d)
    memf = mem.reshape(bsz * mem_len, d)
    n_odd = od_w_in.shape[0]
    moe_gu = moe_w_gu.reshape(n_odd * N_EXPERTS, d, 2 * FFN_DIM)
    moe_dn = moe_w_down.reshape(n_odd * N_EXPERTS, FFN_DIM, d)
    router_w = jnp.pad(moe_router, ((0, 0), (0, 0), (0, LANES - N_EXPERTS)))
    gate_w = jnp.pad(ev_w_in[:, :, EVEN_MAIN:], ((0, 0), (0, 0), (0, LANES - 4 * B_HEADS)))
    conv_w8 = jnp.pad(ev_conv, ((0, 0), (0, 8 - B_CONV), (0, 0)))

    for layer in range(depth):
        j = layer // 2
        if layer % 2 == 0:
            xs = even_mixer(xs, ln_mix[layer], ev_w_in, gate_w, conv_w8[j], ev_b_gate[j], ev_sink[j],
                            ev_head_norm[j], ev_w_out, j, rope, bsz, seq)
        else:
            xs = odd_mixer(xs, ln_mix[layer], od_w_in, od_w_out, j, rope, bsz, seq)
        kv = norm_matmul(memf, ln_mem[layer], x_wkv, layer, 2 * X_W, name="mem_kv")
        last = layer == depth - 1
        if layer % 2 == 0:
            xs, u = cross_attention(xs, ln_xattn[layer], x_wq, x_wo, layer, kv, seq, mem_len, ln_ffn[layer])
            xs = dense_ffn(xs, u, ffn_w_gu, ffn_w_down, j)
            if last:
                xs = rmsnorm(xs, ln_final, F32, name="final_norm")
        else:
            xs = cross_attention(xs, ln_xattn[layer], x_wq, x_wo, layer, kv, seq, mem_len)
            xs = moe_ffn(xs, ln_ffn[layer], router_w, moe_gu, moe_dn, j, ln_final if last else None)
    return xs.reshape(bsz, seq, d)
```

```python
import functools
import math

import jax
import jax.numpy as jnp
from jax import lax
from jax.experimental import pallas as pl
from jax.experimental.pallas import tpu as pltpu

D_MODEL = 2048
HEAD_DIM = 128
A_Q_HEADS = 8
A_KV_HEADS = 2
A_HALF_WINDOW = 128
B_HEADS = 4
B_HEAD_DIM = 256
B_CONV = 5
C_HEADS = D_MODEL // HEAD_DIM
C_PATTERNS = ((128, 1), (512, 4), (2048, 16))
X_HEADS = 4
FFN_DIM = 7168
N_EXPERTS = 8
ROPE_THETA = 500000.0
ROT_DIM = HEAD_DIM // 4
EPS = 1e-6

A_Q_W = A_Q_HEADS * HEAD_DIM
A_KV_W = A_KV_HEADS * HEAD_DIM
B_W = B_HEADS * B_HEAD_DIM
X_W = X_HEADS * HEAD_DIM
EVEN_MAIN = A_Q_W + 2 * A_KV_W + 4 * B_W
COL_KA = A_Q_W
COL_VA = A_Q_W + A_KV_W
COL_QKB = A_Q_W + 2 * A_KV_W
COL_VB = COL_QKB + 2 * B_W
COL_OB = COL_VB + B_W

LANES = 128
V7X_VMEM_BYTES = 64 * 1024 * 1024
VMEM_CAP = V7X_VMEM_BYTES - 8 * 1024 * 1024

NEG = -1e30
BF16 = jnp.bfloat16
F32 = jnp.float32


def _params(sem, vmem_bytes):
    return pltpu.CompilerParams(dimension_semantics=sem, vmem_limit_bytes=int(min(VMEM_CAP, vmem_bytes)))


def _nbytes(shape, dtype):
    return math.prod(shape) * jnp.dtype(dtype).itemsize


def _dot(a, b):
    return jnp.dot(a, b, preferred_element_type=F32)


def _dot_nt(a, b):
    return lax.dot_general(a, b, (((1,), (1,)), ((), ())), preferred_element_type=F32)


def _dot_tn(a, b):
    return lax.dot_general(a, b, (((0,), (0,)), ((), ())), preferred_element_type=F32)


def _rms(x, g):
    return x * lax.rsqrt(jnp.mean(x * x, axis=-1, keepdims=True) + EPS) * g


def _rope_tile(z, c, s1, s2):
    return z * c + pltpu.roll(z, LANES - ROT_DIM // 2, 1) * s1 + pltpu.roll(z, ROT_DIM // 2, 1) * s2


def _norm_mm_kernel(x_ref, g_ref, w_ref, c_ref, s1_ref, s2_ref, o_ref, xn_ref, *, tn, n_steps, n_rope, n_scale,
                    scale):
    j = pl.program_id(1)

    @pl.when(j == 0)
    def _():
        xn_ref[...] = _rms(x_ref[...], g_ref[...]).astype(BF16)

    heads = tn // LANES
    half = min(tn, 2 * LANES)

    def step(kinds):
        xn = xn_ref[...]
        for c0 in range(0, tn, half):
            z = _dot(xn, w_ref[:, c0:c0 + half].astype(BF16))
            for hh in range(c0 // LANES, (c0 + half) // LANES):
                zt = z[:, hh * LANES - c0:(hh + 1) * LANES - c0]
                rotate, scaled = kinds[hh]
                if rotate:
                    zt = _rope_tile(zt, c_ref[...], s1_ref[...], s2_ref[...])
                if scaled:
                    zt = zt * scale
                o_ref[:, hh * LANES:(hh + 1) * LANES] = zt.astype(o_ref.dtype)

    kinds_of = lambda jj: tuple((jj * heads + hh < n_rope, jj * heads + hh < n_scale) for hh in range(heads))
    groups = {}
    for jj in range(n_steps):
        groups.setdefault(kinds_of(jj), []).append(jj)
    for kinds, js in groups.items():
        assert js == list(range(js[0], js[-1] + 1))
        if len(groups) == 1:
            step(kinds)
        else:
            pl.when((j >= js[0]) & (j <= js[-1]))(functools.partial(step, kinds))


def norm_matmul(x, g, w3, widx, n_out, *, rope=None, n_rope=0, n_scale=0, scale=1.0, seq=None, tm=1024, tn=512,
                out_dtype=BF16, name="norm_mm"):
    m, k = x.shape
    tm = min(tm, m)
    tn = min(tn, n_out)
    assert m % tm == 0 and n_out % tn == 0 and tn % LANES == 0
    if rope is None:
        dummy = jnp.zeros((8, LANES), F32)
        rope = (dummy, dummy, dummy)
        rspec = pl.BlockSpec((8, LANES), lambda i, j: (0, 0))
    else:
        nsb = seq // tm
        assert seq % tm == 0
        rspec = pl.BlockSpec((tm, LANES), lambda i, j: (i % nsb, 0))
    vm = 2 * _nbytes((tm, k), F32) + _nbytes((tm, k), BF16) + 2 * _nbytes((k, tn), F32) + _nbytes((k, tn), BF16) \
        + 2 * _nbytes((tm, tn), out_dtype) + 2 * _nbytes((tm, tn), F32) + 6 * _nbytes((tm, LANES), F32) + (4 << 20)
    return pl.pallas_call(
        functools.partial(_norm_mm_kernel, tn=tn, n_steps=n_out // tn, n_rope=n_rope, n_scale=n_scale, scale=scale),
        grid=(m // tm, n_out // tn),
        in_specs=[pl.BlockSpec((tm, k), lambda i, j: (i, 0)),
                  pl.BlockSpec((1, k), lambda i, j: (0, 0)),
                  pl.BlockSpec((None, k, tn), lambda i, j: (widx, 0, j)),
                  rspec, rspec, rspec],
        out_specs=pl.BlockSpec((tm, tn), lambda i, j: (i, j)),
        out_shape=jax.ShapeDtypeStruct((m, n_out), out_dtype),
        scratch_shapes=[pltpu.VMEM((tm, k), BF16)],
        compiler_params=_params(("arbitrary", "arbitrary"), vm),
        name=name,
    )(x, g.reshape(1, k), w3, *rope)


def _mm_resid_kernel(*refs, n_x):
    xs = refs[:n_x]
    ws = refs[n_x:2 * n_x]
    r_ref = refs[2 * n_x]
    o_ref = refs[2 * n_x + 1]
    acc = r_ref[...]
    for x_ref, w_ref in zip(xs, ws):
        acc = acc + _dot(x_ref[...].astype(BF16), w_ref[...].astype(BF16))
    o_ref[...] = acc


def matmul_resid(xs, w3, widx, resid, *, tm=512, tn=1024, name="mm_resid"):
    m, kx = xs[0].shape
    n = w3.shape[2]
    n_x = len(xs)
    assert w3.shape[1] == n_x * kx and m % tm == 0 and n % tn == 0
    in_specs = [pl.BlockSpec((tm, kx), lambda j, i: (i, 0)) for _ in xs]
    in_specs += [pl.BlockSpec((None, kx, tn), lambda j, i, q=q: (widx, q, j)) for q in range(n_x)]
    in_specs += [pl.BlockSpec((tm, tn), lambda j, i: (i, j))]
    vm = n_x * (2 * _nbytes((tm, kx), xs[0].dtype) + 3 * _nbytes((kx, tn), F32)) + 6 * _nbytes((tm, tn), F32) + (4 << 20)
    return pl.pallas_call(
        functools.partial(_mm_resid_kernel, n_x=n_x),
        grid=(n // tn, m // tm),
        in_specs=in_specs,
        out_specs=pl.BlockSpec((tm, tn), lambda j, i: (i, j)),
        out_shape=jax.ShapeDtypeStruct((m, n), F32),
        compiler_params=_params(("arbitrary", "arbitrary"), vm),
        name=name,
    )(*xs, *([w3] * n_x), resid)


def _rmsnorm_kernel(x_ref, g_ref, o_ref):
    o_ref[...] = _rms(x_ref[...], g_ref[...]).astype(o_ref.dtype)


def rmsnorm(x, g, out_dtype, *, tm=512, name="rmsnorm"):
    m, k = x.shape
    tm = min(tm, m)
    assert m % tm == 0
    vm = 4 * _nbytes((tm, k), F32) + 2 * _nbytes((tm, k), out_dtype) + (4 << 20)
    return pl.pallas_call(
        _rmsnorm_kernel,
        grid=(m // tm,),
        in_specs=[pl.BlockSpec((tm, k), lambda i: (i, 0)), pl.BlockSpec((1, k), lambda i: (0, 0))],
        out_specs=pl.BlockSpec((tm, k), lambda i: (i, 0)),
        out_shape=jax.ShapeDtypeStruct((m, k), out_dtype),
        compiler_params=_params(("arbitrary",), vm),
        name=name,
    )(x, g.reshape(1, k))


def _win_attn_kernel(sink_ref, q_ref, kp_ref, kc_ref, kn_ref, vp_ref, vc_ref, vn_ref, o_ref, *, tq, seq, grp):
    i = pl.program_id(1)
    kv = pl.program_id(2)
    hw = A_HALF_WINDOW
    k = jnp.concatenate([kp_ref[...], kc_ref[...], kn_ref[...]], axis=0)
    v = jnp.concatenate([vp_ref[...], vc_ref[...], vn_ref[...]], axis=0)
    wk = tq + 2 * hw
    qpos = i * tq + lax.broadcasted_iota(jnp.int32, (tq, wk), 0)
    kpos = i * tq - hw + lax.broadcasted_iota(jnp.int32, (tq, wk), 1)
    valid = (jnp.abs(kpos - qpos) <= hw) & (kpos >= 0) & (kpos < seq)
    for g in range(grp):
        sk = sink_ref[kv * grp + g]
        s = _dot_nt(q_ref[:, g * HEAD_DIM:(g + 1) * HEAD_DIM], k)
        s = jnp.where(valid, s, NEG)
        m = jnp.maximum(jnp.max(s, axis=1, keepdims=True), sk)
        p = jnp.exp(s - m)
        den = jnp.sum(p, axis=1, keepdims=True) + jnp.exp(sk - m)
        o = _dot(p.astype(BF16), v) / den
        o_ref[:, g * HEAD_DIM:(g + 1) * HEAD_DIM] = o.astype(o_ref.dtype)


def window_attention(z, sink, bsz, seq, *, tq=256, name="win_attn"):
    t = z.shape[0]
    hw = A_HALF_WINDOW
    grp = A_Q_HEADS // A_KV_HEADS
    nq = seq // tq
    r = tq // hw
    nhb = seq // hw
    qw = grp * HEAD_DIM
    kcol = COL_KA // HEAD_DIM
    vcol = COL_VA // HEAD_DIM
    cur = lambda col: pl.BlockSpec((tq, HEAD_DIM), lambda b, i, kv, s: (b * nq + i, col + kv))
    prev = lambda col: pl.BlockSpec((hw, HEAD_DIM), lambda b, i, kv, s: (b * nhb + jnp.maximum(i * r - 1, 0), col + kv))
    nxt = lambda col: pl.BlockSpec((hw, HEAD_DIM), lambda b, i, kv, s: (b * nhb + jnp.minimum((i + 1) * r, nhb - 1), col + kv))
    grid_spec = pltpu.PrefetchScalarGridSpec(
        num_scalar_prefetch=1,
        grid=(bsz, nq, A_KV_HEADS),
        in_specs=[pl.BlockSpec((tq, qw), lambda b, i, kv, s: (b * nq + i, kv)),
                  prev(kcol), cur(kcol), nxt(kcol), prev(vcol), cur(vcol), nxt(vcol)],
        out_specs=pl.BlockSpec((tq, qw), lambda b, i, kv, s: (b * nq + i, kv)),
    )
    return pl.pallas_call(
        functools.partial(_win_attn_kernel, tq=tq, seq=seq, grp=grp),
        grid_spec=grid_spec,
        out_shape=jax.ShapeDtypeStruct((t, A_Q_W), BF16),
        compiler_params=_params(("arbitrary",) * 3, 32 << 20),
        name=name,
    )(sink.astype(F32), z, z, z, z, z, z, z)


def _dilated_bias(tq, wk, reach):
    row = lax.broadcasted_iota(jnp.int32, (tq, wk), 0)
    col = lax.broadcasted_iota(jnp.int32, (tq, wk), 1)
    d = col - reach - row
    ad = jnp.abs(d)
    mult = jnp.zeros((tq, wk), jnp.int32)
    for window, dil in C_PATTERNS:
        mult = mult + ((ad <= window // 2) & ((d & (dil - 1)) == 0)).astype(jnp.int32)
    bias = jnp.where(mult == 1, 0.0, jnp.where(mult == 2, 1.0, math.log2(3.0)))
    return jnp.where(mult == 0, NEG, bias).astype(F32)


def _dil_attn_kernel(q_ref, k_ref, v_ref, o_ref, kp_ref, vp_ref, bias_ref, *, tq, seq, reach, nh, rb, nsub):
    b = pl.program_id(0)
    hg = pl.program_id(1)
    i = pl.program_id(2)
    wk = tq + 2 * reach
    hd = HEAD_DIM

    @pl.when((b == 0) & (hg == 0) & (i == 0))
    def _():
        bias_ref[...] = _dilated_bias(tq, wk, reach)
        kp_ref[...] = jnp.zeros_like(kp_ref)
        vp_ref[...] = jnp.zeros_like(vp_ref)
        lane = lax.broadcasted_iota(jnp.int32, (seq, hd), 1)
        ones_col = jnp.where(lane == 0, 1.0, 0.0).astype(BF16)
        for h in range(nh):
            vp_ref[h, reach:reach + seq, hd:2 * hd] = ones_col

    @pl.when(i == 0)
    def _():
        for h in range(nh):
            kp_ref[h, reach:reach + seq, :] = k_ref[:, h * hd:(h + 1) * hd]
            vp_ref[h, reach:reach + seq, 0:hd] = v_ref[:, h * hd:(h + 1) * hd]

    def attend(edge):
        for w in range(nsub):
            first = (i * nsub + w) * tq
            start = pl.multiple_of(first, tq)
            if edge:
                kpos = first - reach + lax.broadcasted_iota(jnp.int32, (1, wk), 1)
                colbias = jnp.where((kpos >= 0) & (kpos < seq), 0.0, NEG).astype(F32)
            for h in range(nh):
                ks = kp_ref[h, pl.ds(start, wk), :]
                vs = vp_ref[h, pl.ds(start, wk), :]
                for r0 in range(0, tq, rb):
                    q = q_ref[w * tq + r0:w * tq + r0 + rb, h * hd:(h + 1) * hd]
                    s = _dot_nt(q, ks) + bias_ref[r0:r0 + rb, :]
                    if edge:
                        s = s + colbias
                    m = jnp.max(s, axis=1, keepdims=True)
                    p = jnp.exp2(s - m).astype(BF16)
                    pv = _dot(p, vs)
                    o = pv[:, 0:hd] / pv[:, hd:hd + 1]
                    o_ref[w * tq + r0:w * tq + r0 + rb, h * hd:(h + 1) * hd] = o.astype(o_ref.dtype)

    is_edge = (i * nsub * tq < reach) | ((i + 1) * nsub * tq + reach > seq)
    pl.when(is_edge)(lambda: attend(True))
    pl.when(jnp.logical_not(is_edge))(lambda: attend(False))


def dilated_attention(z, bsz, seq, *, tq=256, nh=2, rb=128, nsub=4, name="dil_attn"):
    t = z.shape[0]
    reach = max(w // 2 for w, _ in C_PATTERNS)
    tb = nsub * tq
    nq = seq // tb
    wk = tq + 2 * reach
    ng = C_HEADS // nh
    gw = nh * HEAD_DIM
    assert seq % tb == 0 and seq >= wk and C_HEADS % nh == 0
    vm = 4 * _nbytes((seq, gw), BF16) + 3 * nh * _nbytes((seq + 2 * reach, HEAD_DIM), BF16) \
        + (1 + 3 * nh) * _nbytes((tq, wk), F32) + (8 << 20)
    return pl.pallas_call(
        functools.partial(_dil_attn_kernel, tq=tq, seq=seq, reach=reach, nh=nh, rb=rb, nsub=nsub),
        grid=(bsz, ng, nq),
        in_specs=[pl.BlockSpec((tb, gw), lambda b, g, i: (b * nq + i, g)),
                  pl.BlockSpec((seq, gw), lambda b, g, i: (b, ng + g)),
                  pl.BlockSpec((seq, gw), lambda b, g, i: (b, 2 * ng + g))],
        out_specs=pl.BlockSpec((tb, gw), lambda b, g, i: (b * nq + i, g)),
        out_shape=jax.ShapeDtypeStruct((t, D_MODEL), BF16),
        scratch_shapes=[pltpu.VMEM((nh, seq + 2 * reach, HEAD_DIM), BF16),
                        pltpu.VMEM((nh, seq + 2 * reach, 2 * HEAD_DIM), BF16),
                        pltpu.VMEM((tq, wk), F32)],
        compiler_params=_params(("arbitrary",) * 3, vm),
        name=name,
    )(z, z, z)


def _xattn_kernel(x_ref, g_ref, wq_ref, kv_ref, wo_ref, gn_ref, o_ref, *rest, emit_next):
    wq_s, wo_s = rest[-2:]

    @pl.when(pl.program_id(0) == 0)
    def _():
        wq_s[...] = wq_ref[...].astype(BF16)
        wo_s[...] = wo_ref[...].astype(BF16)

    x = x_ref[...]
    xn = _rms(x, g_ref[...]).astype(BF16)
    q = (_dot(xn, wq_s[...]) * (HEAD_DIM ** -0.5)).astype(BF16)
    kv = kv_ref[...]
    outs = []
    for h in range(X_HEADS):
        k = kv[:, h * HEAD_DIM:(h + 1) * HEAD_DIM]
        v = kv[:, X_W + h * HEAD_DIM:X_W + (h + 1) * HEAD_DIM]
        s = _dot_nt(q[:, h * HEAD_DIM:(h + 1) * HEAD_DIM], k)
        m = jnp.max(s, axis=1, keepdims=True)
        p = jnp.exp(s - m)
        den = jnp.sum(p, axis=1, keepdims=True)
        outs.append((_dot(p.astype(BF16), v) / den).astype(BF16))
    o = jnp.concatenate(outs, axis=1)
    y = x + _dot(o, wo_s[...])
    o_ref[...] = y
    if emit_next:
        rest[0][...] = _rms(y, gn_ref[...]).astype(BF16)


def cross_attention(x, g, wq3, wo3, layer, kv, seq, mem_len, next_gain=None, *, tm=512, name="xattn"):
    t, d = x.shape
    nsb = seq // tm
    assert seq % tm == 0
    emit_next = next_gain is not None
    vm = 4 * _nbytes((tm, d), F32) + 3 * _nbytes((d, X_W), F32) * 2 + 8 * _nbytes((tm, d), F32) + (8 << 20)
    row = pl.BlockSpec((tm, d), lambda i: (i, 0))
    vec = pl.BlockSpec((1, d), lambda i: (0, 0))
    out = pl.pallas_call(
        functools.partial(_xattn_kernel, emit_next=emit_next),
        grid=(t // tm,),
        in_specs=[row, vec,
                  pl.BlockSpec((None, d, X_W), lambda i: (layer, 0, 0)),
                  pl.BlockSpec((mem_len, 2 * X_W), lambda i: (i // nsb, 0)),
                  pl.BlockSpec((None, X_W, d), lambda i: (layer, 0, 0)),
                  vec],
        out_specs=[row, row] if emit_next else [row],
        out_shape=[jax.ShapeDtypeStruct((t, d), F32)] + ([jax.ShapeDtypeStruct((t, d), BF16)] if emit_next else []),
        scratch_shapes=[pltpu.VMEM((d, X_W), BF16), pltpu.VMEM((X_W, d), BF16)],
        compiler_params=_params(("arbitrary",), vm),
        name=name,
    )(x, g.reshape(1, d), wq3, kv, wo3, (next_gain if emit_next else g).reshape(1, d))
    return tuple(out) if emit_next else out[0]


def _conv_kernel(xp_ref, xc_ref, xn_ref, w_ref, o_ref, *, ts, ns, halo, q_scale, n_q_blocks):
    i = pl.program_id(1)
    j = pl.program_id(2)
    pad = B_CONV // 2
    xp = jnp.where(i > 0, xp_ref[halo - pad:, :].astype(F32), 0.0)
    xn = jnp.where(i < ns - 1, xn_ref[:pad, :].astype(F32), 0.0)
    xx = jnp.concatenate([xp, xc_ref[...].astype(F32), xn], axis=0)
    w = w_ref[...]
    acc = xx[0:ts, :] * w[0:1, :]
    for tap in range(1, B_CONV):
        acc = acc + xx[tap:tap + ts, :] * w[tap:tap + 1, :]
    y = acc * jax.nn.sigmoid(acc)
    y = y * jnp.where(j < n_q_blocks, q_scale, 1.0)
    o_ref[...] = y.astype(o_ref.dtype)


def conv_silu(z, conv_w8, bsz, seq, *, ts=512, tc=512, name="conv_silu"):
    t = z.shape[0]
    halo = 16
    ns = seq // ts
    nh = seq // halo
    cb = COL_QKB // tc
    assert COL_QKB % tc == 0 and seq % ts == 0 and B_W % tc == 0
    return pl.pallas_call(
        functools.partial(_conv_kernel, ts=ts, ns=ns, halo=halo, q_scale=B_HEAD_DIM ** -0.5, n_q_blocks=B_W // tc),
        grid=(bsz, ns, 2 * B_W // tc),
        in_specs=[pl.BlockSpec((halo, tc), lambda b, i, j: (b * nh + jnp.maximum(i * (ts // halo) - 1, 0), cb + j)),
                  pl.BlockSpec((ts, tc), lambda b, i, j: (b * ns + i, cb + j)),
                  pl.BlockSpec((halo, tc), lambda b, i, j: (b * nh + jnp.minimum((i + 1) * (ts // halo), nh - 1), cb + j)),
                  pl.BlockSpec((8, tc), lambda b, i, j: (0, j))],
        out_specs=pl.BlockSpec((ts, tc), lambda b, i, j: (b * ns + i, j)),
        out_shape=jax.ShapeDtypeStruct((t, 2 * B_W), BF16),
        compiler_params=_params(("arbitrary",) * 3, 32 << 20),
        name=name,
    )(z, z, z, conv_w8)


def _log_sigmoid(x):
    return jnp.minimum(x, 0.0) - jnp.log(1.0 + jnp.exp(-jnp.abs(x)))


def _mlstm_chunk(q, k, v, i_col, f_col, i_row, f_row, c_ref, n_ref, m_ref, rev):
    L = q.shape[0]
    logf_c = _log_sigmoid(f_col)
    logf_r = _log_sigmoid(f_row)
    row = lax.broadcasted_iota(jnp.int32, (L, L), 0)
    col = lax.broadcasted_iota(jnp.int32, (L, L), 1)
    causal = (col >= row) if rev else (col <= row)
    causal_t = (row >= col) if rev else (row <= col)
    b_col = jnp.sum(jnp.where(causal, logf_r, 0.0), axis=1, keepdims=True)
    b_row = jnp.sum(jnp.where(causal_t, logf_c, 0.0), axis=0, keepdims=True)
    b_all = jnp.sum(logf_r, axis=1, keepdims=True)
    m_st = m_ref[...]
    c_st = c_ref[...]
    n_st = n_ref[...]
    logd = jnp.where(causal, b_col - b_row + i_row, NEG)
    m_inter = b_col + m_st
    mt = jnp.maximum(m_inter, jnp.max(logd, axis=1, keepdims=True))
    s = _dot_nt(q, k) * jnp.exp(logd - mt)
    sc = jnp.exp(m_inter - mt)
    num = _dot(s.astype(BF16), v) + sc * _dot(q, c_st.astype(BF16))
    den = jnp.sum(s, axis=1, keepdims=True) + sc * jnp.sum(q.astype(F32) * n_st, axis=1, keepdims=True)
    h = num / jnp.maximum(jnp.abs(den), jnp.exp(-mt))
    logw = b_all - b_col + i_col
    m_new = jnp.maximum(b_all + m_st, jnp.max(logw, axis=0, keepdims=True))
    wgt = jnp.exp(logw - m_new)
    dec = jnp.exp(b_all + m_st - m_new)
    kw = k.astype(F32) * wgt
    c_ref[...] = dec * c_st + _dot_tn(kw.astype(BF16), v)
    n_ref[...] = dec * n_st + jnp.sum(kw, axis=0, keepdims=True)
    m_ref[...] = m_new
    return h


def _mlstm_kernel(bias_ref, qf_ref, kf_ref, vf0_ref, vf1_ref, gcf_ref, grf_ref,
                  qb_ref, kb_ref, vb0_ref, vb1_ref, gcb_ref, grb_ref,
                  hf_ref, hb_ref, c_ref, n_ref, m_ref):
    @pl.when(pl.program_id(1) == 0)
    def _():
        c_ref[...] = jnp.zeros_like(c_ref)
        n_ref[...] = jnp.zeros_like(n_ref)
        m_ref[...] = jnp.zeros_like(m_ref)

    lane = lax.broadcasted_iota(jnp.int32, (1, LANES), 1)
    dh = B_HEAD_DIM
    for d, (q_ref, k_ref, v0_ref, v1_ref, gc_ref, gr_ref, h_ref) in enumerate((
            (qf_ref, kf_ref, vf0_ref, vf1_ref, gcf_ref, grf_ref, hf_ref),
            (qb_ref, kb_ref, vb0_ref, vb1_ref, gcb_ref, grb_ref, hb_ref))):
        gc = gc_ref[...]
        for hd in range(B_HEADS):
            ci = 2 * d * B_HEADS + hd
            cf = ci + B_HEADS
            i_col = jnp.sum(jnp.where(lane == ci, gc, 0.0), axis=1, keepdims=True) + bias_ref[ci]
            f_col = jnp.sum(jnp.where(lane == cf, gc, 0.0), axis=1, keepdims=True) + bias_ref[cf]
            i_row = gr_ref[ci:ci + 1, :] + bias_ref[ci]
            f_row = gr_ref[cf:cf + 1, :] + bias_ref[cf]
            v_ref = v0_ref if hd < B_HEADS // 2 else v1_ref
            vo = (hd % (B_HEADS // 2)) * dh
            st = d * B_HEADS + hd
            h = _mlstm_chunk(q_ref[:, hd * dh:(hd + 1) * dh], k_ref[:, hd * dh:(hd + 1) * dh],
                             v_ref[:, vo:vo + dh], i_col, f_col, i_row, f_row,
                             c_ref.at[st], n_ref.at[st], m_ref.at[st], rev=bool(d))
            h_ref[:, hd * dh:(hd + 1) * dh] = h


def mlstm(qk, z, g_cols, g_rows, b_gate, bsz, seq, *, chunk=256, name="mlstm"):
    t = qk.shape[0]
    nc = seq // chunk
    assert seq % chunk == 0
    half = B_W // 2
    vcol = COL_VB // half
    assert COL_VB % half == 0
    fw = lambda b, c, s: b * nc + c
    bw = lambda b, c, s: b * nc + (nc - 1 - c)

    def specs(rowf):
        return [pl.BlockSpec((chunk, B_W), lambda b, c, s: (rowf(b, c, s), 0)),
                pl.BlockSpec((chunk, B_W), lambda b, c, s: (rowf(b, c, s), 1)),
                pl.BlockSpec((chunk, half), lambda b, c, s: (rowf(b, c, s), vcol)),
                pl.BlockSpec((chunk, half), lambda b, c, s: (rowf(b, c, s), vcol + 1)),
                pl.BlockSpec((chunk, LANES), lambda b, c, s: (rowf(b, c, s), 0)),
                pl.BlockSpec((16, chunk), lambda b, c, s: (0, rowf(b, c, s)))]

    grid_spec = pltpu.PrefetchScalarGridSpec(
        num_scalar_prefetch=1,
        grid=(bsz, nc),
        in_specs=specs(fw) + specs(bw),
        out_specs=[pl.BlockSpec((chunk, B_W), lambda b, c, s: (fw(b, c, s), 0)),
                   pl.BlockSpec((chunk, B_W), lambda b, c, s: (bw(b, c, s), 0))],
        scratch_shapes=[pltpu.VMEM((2 * B_HEADS, B_HEAD_DIM, B_HEAD_DIM), F32),
                        pltpu.VMEM((2 * B_HEADS, 1, B_HEAD_DIM), F32),
                        pltpu.VMEM((2 * B_HEADS, 1, 1), F32)],
    )
    args = (qk, qk, z, z, g_cols, g_rows)
    return pl.pallas_call(
        _mlstm_kernel,
        grid_spec=grid_spec,
        out_shape=[jax.ShapeDtypeStruct((t, B_W), F32), jax.ShapeDtypeStruct((t, B_W), F32)],
        compiler_params=_params(("arbitrary", "arbitrary"), 48 << 20),
        name=name,
    )(b_gate.astype(F32), *args, *args)


def _mlstm_out_kernel(hf_ref, hb_ref, o0_ref, o1_ref, gain_ref, y_ref):
    dh = B_HEAD_DIM
    for hd in range(B_HEADS):
        sl = slice(hd * dh, (hd + 1) * dh)
        h = hf_ref[:, sl] + hb_ref[:, sl]
        hc = h - jnp.mean(h, axis=1, keepdims=True)
        y = hc * lax.rsqrt(jnp.mean(hc * hc, axis=1, keepdims=True) + EPS) * gain_ref[:, sl]
        o_ref = o0_ref if hd < B_HEADS // 2 else o1_ref
        oo = (hd % (B_HEADS // 2)) * dh
        y_ref[:, sl] = (jax.nn.sigmoid(o_ref[:, oo:oo + dh].astype(F32)) * y).astype(y_ref.dtype)


def mlstm_output(hf, hb, z, gain, *, tm=512, name="mlstm_out"):
    t = hf.shape[0]
    half = B_W // 2
    ocol = COL_OB // half
    assert COL_OB % half == 0 and t % tm == 0
    return pl.pallas_call(
        _mlstm_out_kernel,
        grid=(t // tm,),
        in_specs=[pl.BlockSpec((tm, B_W), lambda i: (i, 0)),
                  pl.BlockSpec((tm, B_W), lambda i: (i, 0)),
                  pl.BlockSpec((tm, half), lambda i: (i, ocol)),
                  pl.BlockSpec((tm, half), lambda i: (i, ocol + 1)),
                  pl.BlockSpec((1, B_W), lambda i: (0, 0))],
        out_specs=pl.BlockSpec((tm, B_W), lambda i: (i, 0)),
        out_shape=jax.ShapeDtypeStruct((t, B_W), BF16),
        compiler_params=_params(("arbitrary",), 32 << 20),
        name=name,
    )(hf, hb, z, z, gain.reshape(1, B_W))


def _tile_state(na_ref, th_ref):
    i = pl.program_id(1)
    active = i < na_ref[0]
    half_only = th_ref[jnp.minimum(i, na_ref[0] - 1)] == 1
    return active, half_only


def _up_kernel(te_ref, na_ref, th_ref, x_ref, wg_ref, wu_ref, h_ref, *, tm):
    active, half_only = _tile_state(na_ref, th_ref)

    def compute(rows):
        x = x_ref[0:rows, :]
        g = _dot(x, wg_ref[...].astype(BF16))
        u = _dot(x, wu_ref[...].astype(BF16))
        h_ref[0:rows, :] = (g * jax.nn.sigmoid(g) * u).astype(h_ref.dtype)
        if rows < tm:
            h_ref[rows:, :] = jnp.zeros((tm - rows, h_ref.shape[1]), h_ref.dtype)

    pl.when(active & jnp.logical_not(half_only))(lambda: compute(tm))
    pl.when(active & half_only)(lambda: compute(tm // 2))

    @pl.when(jnp.logical_not(active))
    def _():
        h_ref[...] = jnp.zeros_like(h_ref)


def swiglu_up(x, w_gu3, tile_expert, n_active, tile_half, *, tm=512, tf=512, name="ffn_up"):
    p, d = x.shape
    f = w_gu3.shape[2] // 2
    nt = p // tm
    nj = f // tf
    assert p % tm == 0 and f % tf == 0
    row = lambda i, na: jnp.minimum(i, na[0] - 1)
    grid_spec = pltpu.PrefetchScalarGridSpec(
        num_scalar_prefetch=3,
        grid=(nj, nt),
        in_specs=[pl.BlockSpec((tm, d), lambda j, i, te, na, th: (row(i, na), 0)),
                  pl.BlockSpec((None, d, tf), lambda j, i, te, na, th: (te[row(i, na)], 0, j)),
                  pl.BlockSpec((None, d, tf), lambda j, i, te, na, th: (te[row(i, na)], 0, nj + j))],
        out_specs=pl.BlockSpec((tm, tf), lambda j, i, te, na, th: (i, j)),
    )
    vm = 2 * _nbytes((tm, d), BF16) + 2 * 3 * _nbytes((d, tf), F32) + 8 * _nbytes((tm, tf), F32) + (4 << 20)
    return pl.pallas_call(
        functools.partial(_up_kernel, tm=tm),
        grid_spec=grid_spec,
        out_shape=jax.ShapeDtypeStruct((p, f), BF16),
        compiler_params=_params(("arbitrary", "arbitrary"), vm),
        name=name,
    )(tile_expert, n_active, tile_half, x, w_gu3, w_gu3)


def _down_kernel(te_ref, na_ref, th_ref, h_ref, w_ref, *rest, tm, has_resid):
    o_ref = rest[-1]
    active, half_only = _tile_state(na_ref, th_ref)

    def compute(rows):
        o = _dot(h_ref[0:rows, :], w_ref[...].astype(BF16))
        if has_resid:
            o = o + rest[0][0:rows, :]
        o_ref[0:rows, :] = o
        if rows < tm:
            o_ref[rows:, :] = jnp.zeros((tm - rows, o_ref.shape[1]), o_ref.dtype)

    pl.when(active & jnp.logical_not(half_only))(lambda: compute(tm))
    pl.when(active & half_only)(lambda: compute(tm // 2))

    @pl.when(jnp.logical_not(active))
    def _():
        o_ref[...] = jnp.zeros_like(o_ref)


def swiglu_down(h, w_d3, tile_expert, n_active, tile_half, resid=None, *, tm=512, tn=512, name="ffn_down"):
    p, f = h.shape
    d = w_d3.shape[2]
    nt = p // tm
    assert p % tm == 0 and d % tn == 0
    row = lambda i, na: jnp.minimum(i, na[0] - 1)
    in_specs = [pl.BlockSpec((tm, f), lambda n, i, te, na, th: (row(i, na), 0)),
                pl.BlockSpec((None, f, tn), lambda n, i, te, na, th: (te[row(i, na)], 0, n))]
    args = [h, w_d3]
    if resid is not None:
        in_specs.append(pl.BlockSpec((tm, tn), lambda n, i, te, na, th: (row(i, na), n)))
        args.append(resid)
    grid_spec = pltpu.PrefetchScalarGridSpec(
        num_scalar_prefetch=3,
        grid=(d // tn, nt),
        in_specs=in_specs,
        out_specs=pl.BlockSpec((tm, tn), lambda n, i, te, na, th: (i, n)),
    )
    vm = 2 * _nbytes((tm, f), BF16) + 2 * _nbytes((f, tn), F32) + _nbytes((f, tn), BF16) + 8 * _nbytes((tm, tn), F32) + (4 << 20)
    return pl.pallas_call(
        functools.partial(_down_kernel, tm=tm, has_resid=resid is not None),
        grid_spec=grid_spec,
        out_shape=jax.ShapeDtypeStruct((p, d), F32),
        compiler_params=_params(("arbitrary", "arbitrary"), vm),
        name=name,
    )(tile_expert, n_active, tile_half, *args)


def _router_kernel(x_ref, g_ref, w_ref, mi_ref, mf_ref, cnt_ref, carry_ref, *, tm):
    @pl.when(pl.program_id(0) == 0)
    def _():
        carry_ref[...] = jnp.zeros_like(carry_ref)

    u = _rms(x_ref[...], g_ref[...])
    logits = jnp.dot(u, w_ref[...], preferred_element_type=F32, precision=lax.Precision.HIGHEST)
    lane = lax.broadcasted_iota(jnp.int32, (tm, LANES), 1)
    logits = jnp.where(lane < N_EXPERTS, logits, -jnp.inf)
    v1 = jnp.max(logits, axis=1, keepdims=True)
    i1 = jnp.min(jnp.where(logits == v1, lane, LANES), axis=1, keepdims=True)
    oh1 = lane == i1
    rest = jnp.where(oh1, -jnp.inf, logits)
    v2 = jnp.max(rest, axis=1, keepdims=True)
    i2 = jnp.min(jnp.where(rest == v2, lane, LANES), axis=1, keepdims=True)
    oh2 = lane == i2
    e = jnp.exp(v2 - v1)
    g1 = 1.0 / (1.0 + e)
    g2 = e * g1
    cnt = (oh1 | oh2).astype(F32)
    r = lax.broadcasted_iota(jnp.int32, (tm, tm), 0)
    c = lax.broadcasted_iota(jnp.int32, (tm, tm), 1)
    before = (c < r).astype(BF16)
    excl = _dot(before, cnt.astype(BF16)) + carry_ref[0:1, :]
    r1 = jnp.sum(jnp.where(oh1, excl, 0.0), axis=1, keepdims=True).astype(jnp.int32)
    r2 = jnp.sum(jnp.where(oh2, excl, 0.0), axis=1, keepdims=True).astype(jnp.int32)
    mi_ref[...] = jnp.where(lane == 0, i1, jnp.where(lane == 1, i2, jnp.where(lane == 2, r1, jnp.where(lane == 3, r2, 0))))
    mf_ref[...] = jnp.where(lane == 0, g1, jnp.where(lane == 1, g2, 0.0))
    carry_ref[...] = carry_ref[...] + jnp.sum(cnt, axis=0, keepdims=True)
    cnt_ref[...] = carry_ref[...]


def moe_router(x, g, w_router3, layer, *, tm=512, name="moe_router"):
    t, d = x.shape
    assert t % tm == 0
    vm = 6 * _nbytes((tm, d), F32) + 4 * _nbytes((d, LANES), F32) + 8 * _nbytes((tm, tm), F32) + (4 << 20)
    return pl.pallas_call(
        functools.partial(_router_kernel, tm=tm),
        grid=(t // tm,),
        in_specs=[pl.BlockSpec((tm, d), lambda i: (i, 0)),
                  pl.BlockSpec((1, d), lambda i: (0, 0)),
                  pl.BlockSpec((None, d, LANES), lambda i: (layer, 0, 0))],
        out_specs=[pl.BlockSpec((tm, LANES), lambda i: (i, 0)),
                   pl.BlockSpec((tm, LANES), lambda i: (i, 0)),
                   pl.BlockSpec((8, LANES), lambda i: (0, 0))],
        out_shape=[jax.ShapeDtypeStruct((t, LANES), jnp.int32),
                   jax.ShapeDtypeStruct((t, LANES), F32),
                   jax.ShapeDtypeStruct((8, LANES), F32)],
        scratch_shapes=[pltpu.VMEM((8, LANES), F32)],
        compiler_params=_params(("arbitrary",), vm),
        name=name,
    )(x, g.reshape(1, d), w_router3)


def _row_copy(src_hbm, dst_vmem, sem, src_row, dst_row):
    return pltpu.make_async_copy(src_hbm.at[pl.ds(src_row, 1), :], dst_vmem.at[pl.ds(dst_row, 1), :], sem)


GATHER_UNROLL = 8


def _start_rows(src_hbm, idx_refs, dst_refs, sem, n):
    def body(g, carry):
        for u in range(GATHER_UNROLL):
            r = g * GATHER_UNROLL + u
            for idx_ref, dst in zip(idx_refs, dst_refs):
                _row_copy(src_hbm, dst, sem, idx_ref[0, 0, r], r).start(priority=u % 2)
        return carry

    assert n % GATHER_UNROLL == 0
    lax.fori_loop(0, n // GATHER_UNROLL, body, 0)


def _wait_rows(src_hbm, dst_refs, sem, n):
    def body(r, carry):
        for dst in dst_refs:
            _row_copy(src_hbm, dst, sem, 0, r).wait()
        return carry

    lax.fori_loop(0, n, body, 0, unroll=GATHER_UNROLL)


def _dispatch_kernel(tok_ref, tok_next_ref, x_hbm, g_ref, o_ref, rows_ref, sems, *, tg):
    i = pl.program_id(0)
    slot = i % 2

    @pl.when(i == 0)
    def _():
        _start_rows(x_hbm, [tok_ref], [rows_ref.at[0]], sems.at[0], tg)

    @pl.when(i + 1 < pl.num_programs(0))
    def _():
        _start_rows(x_hbm, [tok_next_ref], [rows_ref.at[1 - slot]], sems.at[1 - slot], tg)

    _wait_rows(x_hbm, [rows_ref.at[slot]], sems.at[slot], tg)
    o_ref[...] = _rms(rows_ref[slot], g_ref[...]).astype(o_ref.dtype)


def moe_dispatch(x, g, token_of, *, tg=256, name="moe_dispatch"):
    t, d = x.shape
    p = token_of.shape[0]
    assert p % tg == 0
    nt = p // tg
    vm = 4 * _nbytes((tg, d), F32) + 2 * _nbytes((tg, d), BF16) + (4 << 20)
    tok3 = token_of.reshape(nt, 1, tg)
    return pl.pallas_call(
        functools.partial(_dispatch_kernel, tg=tg),
        grid=(nt,),
        in_specs=[pl.BlockSpec((1, 1, tg), lambda i: (i, 0, 0), memory_space=pltpu.SMEM),
                  pl.BlockSpec((1, 1, tg), lambda i: (jnp.minimum(i + 1, nt - 1), 0, 0), memory_space=pltpu.SMEM),
                  pl.BlockSpec(memory_space=pl.ANY),
                  pl.BlockSpec((1, d), lambda i: (0, 0))],
        out_specs=pl.BlockSpec((tg, d), lambda i: (i, 0)),
        out_shape=jax.ShapeDtypeStruct((p, d), BF16),
        scratch_shapes=[pltpu.VMEM((2, tg, d), F32), pltpu.SemaphoreType.DMA((2,))],
        compiler_params=_params(("arbitrary",), vm),
        name=name,
    )(tok3, tok3, x, g.reshape(1, d))


def _combine_kernel(p1_ref, p2_ref, p1n_ref, p2n_ref, o_hbm, mf_ref, x_ref, gf_ref, y_ref, rows_ref, sems, *, tc,
                    final_norm):
    i = pl.program_id(0)
    slot = i % 2
    dst = lambda s: [rows_ref.at[s, 0], rows_ref.at[s, 1]]

    @pl.when(i == 0)
    def _():
        _start_rows(o_hbm, [p1_ref, p2_ref], dst(0), sems.at[0], tc)

    @pl.when(i + 1 < pl.num_programs(0))
    def _():
        _start_rows(o_hbm, [p1n_ref, p2n_ref], dst(1 - slot), sems.at[1 - slot], tc)

    _wait_rows(o_hbm, dst(slot), sems.at[slot], tc)
    mf = mf_ref[...]
    y = x_ref[...] + mf[:, 0:1] * rows_ref[slot, 0] + mf[:, 1:2] * rows_ref[slot, 1]
    y_ref[...] = _rms(y, gf_ref[...]) if final_norm else y


def moe_combine(o, pos1, pos2, gates, x, final_gain=None, *, tc=256, name="moe_combine"):
    t, d = x.shape
    final_norm = final_gain is not None
    assert t % tc == 0
    nt = t // tc
    vm = 4 * _nbytes((tc, d), F32) + 6 * _nbytes((tc, d), F32) + (4 << 20)
    cur = pl.BlockSpec((1, 1, tc), lambda i: (i, 0, 0), memory_space=pltpu.SMEM)
    nxt = pl.BlockSpec((1, 1, tc), lambda i: (jnp.minimum(i + 1, nt - 1), 0, 0), memory_space=pltpu.SMEM)
    p1 = pos1.reshape(nt, 1, tc)
    p2 = pos2.reshape(nt, 1, tc)
    return pl.pallas_call(
        functools.partial(_combine_kernel, tc=tc, final_norm=final_norm),
        grid=(nt,),
        in_specs=[cur, cur, nxt, nxt,
                  pl.BlockSpec(memory_space=pl.ANY),
                  pl.BlockSpec((tc, LANES), lambda i: (i, 0)),
                  pl.BlockSpec((tc, d), lambda i: (i, 0)),
                  pl.BlockSpec((1, d), lambda i: (0, 0))],
        out_specs=pl.BlockSpec((tc, d), lambda i: (i, 0)),
        out_shape=jax.ShapeDtypeStruct((t, d), F32),
        scratch_shapes=[pltpu.VMEM((2, 2, tc, d), F32), pltpu.SemaphoreType.DMA((2,))],
        compiler_params=_params(("arbitrary",), vm),
        name=name,
    )(p1, p2, p1, p2, o, gates, x, (final_gain if final_norm else jnp.ones((d,), F32)).reshape(1, d))


def _rope_tables(seq):
    pos = jnp.arange(seq, dtype=F32)
    inv = jnp.power(ROPE_THETA, -jnp.arange(0, ROT_DIM, 2, dtype=F32) / ROT_DIM)
    ang = pos[:, None] * inv[None, :]
    cos, sin = jnp.cos(ang), jnp.sin(ang)
    half = ROT_DIM // 2
    zeros = jnp.zeros((seq, LANES - ROT_DIM), F32)
    c = jnp.concatenate([cos, cos, jnp.ones_like(zeros)], axis=1)
    s1 = jnp.concatenate([-sin, jnp.zeros((seq, half), F32), zeros], axis=1)
    s2 = jnp.concatenate([jnp.zeros((seq, half), F32), sin, zeros], axis=1)
    return c, s1, s2


def _moe_plan(meta_i, counts, t, tm):
    eid = meta_i[:, 0:2]
    rank = meta_i[:, 2:4]
    cnt = counts[0, :N_EXPERTS].astype(jnp.int32)
    padded = ((cnt + tm - 1) // tm) * tm
    ends = jnp.cumsum(padded)
    starts = ends - padded
    pos = starts[eid] + rank
    p_max = 2 * t + N_EXPERTS * tm
    nt = p_max // tm
    n_active = jnp.maximum(ends[-1] // tm, 1)
    tile_start = jnp.minimum(jnp.arange(nt, dtype=jnp.int32), n_active - 1) * tm
    tile_expert = jnp.minimum(jnp.searchsorted(ends, tile_start, side="right"), N_EXPERTS - 1).astype(jnp.int32)
    tok = jnp.broadcast_to(jnp.arange(t, dtype=jnp.int32)[:, None], (t, 2))
    token_of = jnp.zeros((p_max,), jnp.int32).at[pos.reshape(-1)].set(tok.reshape(-1))
    valid = jnp.clip((starts + cnt)[tile_expert] - tile_start, 0, tm)
    tile_half = (valid <= tm // 2).astype(jnp.int32)
    return pos[:, 0], pos[:, 1], token_of, tile_expert, n_active.reshape(1).astype(jnp.int32), tile_half


FFN_TILE = 512
MOE_UP_COLS = 1024
DENSE_UP_TILE = 1024


def even_mixer(xs, g, w_in3, gate_w3, conv_w8, b_gate, sink, head_gain, w_out3, j, rope, bsz, seq):
    z = norm_matmul(xs, g, w_in3, j, EVEN_MAIN, rope=rope, seq=seq, n_rope=(A_Q_W + A_KV_W) // HEAD_DIM,
                    n_scale=A_Q_W // HEAD_DIM, scale=HEAD_DIM ** -0.5, name="even_in")
    g_cols = norm_matmul(xs, g, gate_w3, j, LANES, out_dtype=F32, name="even_gates")
    g_rows = g_cols[:, :4 * B_HEADS].T
    ya = window_attention(z, sink, bsz, seq)
    qk = conv_silu(z, conv_w8, bsz, seq)
    hf, hb = mlstm(qk, z, g_cols, g_rows, b_gate, bsz, seq)
    yb = mlstm_output(hf, hb, z, head_gain)
    return matmul_resid([ya, yb], w_out3, j, xs, name="even_out")


def odd_mixer(xs, g, w_in3, w_out3, j, rope, bsz, seq):
    d = xs.shape[1]
    z = norm_matmul(xs, g, w_in3, j, 3 * d, rope=rope, seq=seq, n_rope=2 * C_HEADS, n_scale=C_HEADS,
                    scale=HEAD_DIM ** -0.5 * math.log2(math.e), name="odd_in")
    y = dilated_attention(z, bsz, seq)
    return matmul_resid([y], w_out3, j, xs, name="odd_out")


def dense_ffn(xs, u, w_gu3, w_d3, j):
    t = xs.shape[0]
    plan = lambda tm: (jnp.full((t // tm,), j, jnp.int32), jnp.full((1,), t // tm, jnp.int32),
                       jnp.zeros((t // tm,), jnp.int32))
    h = swiglu_up(u, w_gu3, *plan(DENSE_UP_TILE), tm=DENSE_UP_TILE, name="ffn_up")
    return swiglu_down(h, w_d3, *plan(FFN_TILE), resid=xs, tm=FFN_TILE, name="ffn_down")


def moe_ffn(xs, g, router_w3, w_gu3, w_d3, j, final_gain=None):
    t = xs.shape[0]
    meta_i, gates, counts = moe_router(xs, g, router_w3, j)
    pos1, pos2, token_of, tile_expert, n_active, tile_half = _moe_plan(meta_i, counts, t, FFN_TILE)
    xd = moe_dispatch(xs, g, token_of)
    experts = tile_expert + j * N_EXPERTS
    h = swiglu_up(xd, w_gu3, experts, n_active, tile_half, tm=FFN_TILE, tf=MOE_UP_COLS, name="moe_up")
    o = swiglu_down(h, w_d3, experts, n_active, tile_half, tm=FFN_TILE, name="moe_down")
    return moe_combine(o, pos1, pos2, gates, xs, final_gain)


def kernel(x, mem, ln_mix, ln_xattn, ln_mem, ln_ffn, ln_final, ev_w_in, ev_b_gate, ev_conv, ev_sink, ev_head_norm,
           ev_w_out, ffn_w_gu, ffn_w_down, od_w_in, od_w_out, moe_router, moe_w_gu, moe_w_down, x_wq, x_wkv, x_wo):
    bsz, seq, d = x.shape
    mem_len = mem.shape[1]
    depth = ln_mix.shape[0]
    t = bsz * seq
    rope = _rope_tables(seq)
    xs = x.reshape(t, d)
    memf = mem.reshape(bsz * mem_len, d)
    n_odd = od_w_in.shape[0]
    moe_gu = moe_w_gu.reshape(n_odd * N_EXPERTS, d, 2 * FFN_DIM)
    moe_dn = moe_w_down.reshape(n_odd * N_EXPERTS, FFN_DIM, d)
    router_w = jnp.pad(moe_router, ((0, 0), (0, 0), (0, LANES - N_EXPERTS)))
    gate_w = jnp.pad(ev_w_in[:, :, EVEN_MAIN:], ((0, 0), (0, 0), (0, LANES - 4 * B_HEADS)))
    conv_w8 = jnp.pad(ev_conv, ((0, 0), (0, 8 - B_CONV), (0, 0)))

    for layer in range(depth):
        j = layer // 2
        if layer % 2 == 0:
            xs = even_mixer(xs, ln_mix[layer], ev_w_in, gate_w, conv_w8[j], ev_b_gate[j], ev_sink[j],
                            ev_head_norm[j], ev_w_out, j, rope, bsz, seq)
        else:
            xs = odd_mixer(xs, ln_mix[layer], od_w_in, od_w_out, j, rope, bsz, seq)
        kv = norm_matmul(memf, ln_mem[layer], x_wkv, layer, 2 * X_W, name="mem_kv")
        last = layer == depth - 1
        if layer % 2 == 0:
            xs, u = cross_attention(xs, ln_xattn[layer], x_wq, x_wo, layer, kv, seq, mem_len, ln_ffn[layer])
            xs = dense_ffn(xs, u, ffn_w_gu, ffn_w_down, j)
            if last:
                xs = rmsnorm(xs, ln_final, F32, name="final_norm")
        else:
            xs = cross_attention(xs, ln_xattn[layer], x_wq, x_wo, layer, kv, seq, mem_len)
            xs = moe_ffn(xs, ln_ffn[layer], router_w, moe_gu, moe_dn, j, ln_final if last else None)
    return xs.reshape(bsz, seq, d)
```

```python
import functools
import math

import jax
import jax.numpy as jnp
from jax import lax
from jax.experimental import pallas as pl
from jax.experimental.pallas import tpu as pltpu

D_MODEL = 2048
HEAD_DIM = 128
A_Q_HEADS = 8
A_KV_HEADS = 2
A_HALF_WINDOW = 128
B_HEADS = 4
B_HEAD_DIM = 256
B_CONV = 5
C_HEADS = D_MODEL // HEAD_DIM
C_PATTERNS = ((128, 1), (512, 4), (2048, 16))
X_HEADS = 4
FFN_DIM = 7168
N_EXPERTS = 8
ROPE_THETA = 500000.0
ROT_DIM = HEAD_DIM // 4
EPS = 1e-6

A_Q_W = A_Q_HEADS * HEAD_DIM
A_KV_W = A_KV_HEADS * HEAD_DIM
B_W = B_HEADS * B_HEAD_DIM
X_W = X_HEADS * HEAD_DIM
EVEN_MAIN = A_Q_W + 2 * A_KV_W + 4 * B_W
COL_KA = A_Q_W
COL_VA = A_Q_W + A_KV_W
COL_QKB = A_Q_W + 2 * A_KV_W
COL_VB = COL_QKB + 2 * B_W
COL_OB = COL_VB + B_W

LANES = 128
V7X_VMEM_BYTES = 64 * 1024 * 1024
VMEM_CAP = V7X_VMEM_BYTES - 8 * 1024 * 1024

NEG = -1e30
BF16 = jnp.bfloat16
F32 = jnp.float32


def _params(sem, vmem_bytes):
    return pltpu.CompilerParams(dimension_semantics=sem, vmem_limit_bytes=int(min(VMEM_CAP, vmem_bytes)))


def _nbytes(shape, dtype):
    return math.prod(shape) * jnp.dtype(dtype).itemsize


def _dot(a, b):
    return jnp.dot(a, b, preferred_element_type=F32)


def _dot_nt(a, b):
    return lax.dot_general(a, b, (((1,), (1,)), ((), ())), preferred_element_type=F32)


def _dot_tn(a, b):
    return lax.dot_general(a, b, (((0,), (0,)), ((), ())), preferred_element_type=F32)


def _rms(x, g):
    return x * lax.rsqrt(jnp.mean(x * x, axis=-1, keepdims=True) + EPS) * g


def _rope_tile(z, c, s1, s2):
    return z * c + pltpu.roll(z, LANES - ROT_DIM // 2, 1) * s1 + pltpu.roll(z, ROT_DIM // 2, 1) * s2


def _norm_mm_kernel(x_ref, g_ref, w_ref, c_ref, s1_ref, s2_ref, o_ref, xn_ref, *, tn, n_steps, n_rope, n_scale,
                    scale):
    j = pl.program_id(1)

    @pl.when(j == 0)
    def _():
        xn_ref[...] = _rms(x_ref[...], g_ref[...]).astype(BF16)

    heads = tn // LANES
    half = min(tn, 2 * LANES)

    def step(kinds):
        xn = xn_ref[...]
        for c0 in range(0, tn, half):
            z = _dot(xn, w_ref[:, c0:c0 + half].astype(BF16))
            for hh in range(c0 // LANES, (c0 + half) // LANES):
                zt = z[:, hh * LANES - c0:(hh + 1) * LANES - c0]
                rotate, scaled = kinds[hh]
                if rotate:
                    zt = _rope_tile(zt, c_ref[...], s1_ref[...], s2_ref[...])
                if scaled:
                    zt = zt * scale
                o_ref[:, hh * LANES:(hh + 1) * LANES] = zt.astype(o_ref.dtype)

    kinds_of = lambda jj: tuple((jj * heads + hh < n_rope, jj * heads + hh < n_scale) for hh in range(heads))
    groups = {}
    for jj in range(n_steps):
        groups.setdefault(kinds_of(jj), []).append(jj)
    for kinds, js in groups.items():
        assert js == list(range(js[0], js[-1] + 1))
        if len(groups) == 1:
            step(kinds)
        else:
            pl.when((j >= js[0]) & (j <= js[-1]))(functools.partial(step, kinds))


def norm_matmul(x, g, w3, widx, n_out, *, rope=None, n_rope=0, n_scale=0, scale=1.0, seq=None, tm=1024, tn=512,
                out_dtype=BF16, name="norm_mm"):
    m, k = x.shape
    tm = min(tm, m)
    tn = min(tn, n_out)
    assert m % tm == 0 and n_out % tn == 0 and tn % LANES == 0
    if rope is None:
        dummy = jnp.zeros((8, LANES), F32)
        rope = (dummy, dummy, dummy)
        rspec = pl.BlockSpec((8, LANES), lambda i, j: (0, 0))
    else:
        nsb = seq // tm
        assert seq % tm == 0
        rspec = pl.BlockSpec((tm, LANES), lambda i, j: (i % nsb, 0))
    vm = 2 * _nbytes((tm, k), F32) + _nbytes((tm, k), BF16) + 2 * _nbytes((k, tn), F32) + _nbytes((k, tn), BF16) \
        + 2 * _nbytes((tm, tn), out_dtype) + 2 * _nbytes((tm, tn), F32) + 6 * _nbytes((tm, LANES), F32) + (4 << 20)
    return pl.pallas_call(
        functools.partial(_norm_mm_kernel, tn=tn, n_steps=n_out // tn, n_rope=n_rope, n_scale=n_scale, scale=scale),
        grid=(m // tm, n_out // tn),
        in_specs=[pl.BlockSpec((tm, k), lambda i, j: (i, 0)),
                  pl.BlockSpec((1, k), lambda i, j: (0, 0)),
                  pl.BlockSpec((None, k, tn), lambda i, j: (widx, 0, j)),
                  rspec, rspec, rspec],
        out_specs=pl.BlockSpec((tm, tn), lambda i, j: (i, j)),
        out_shape=jax.ShapeDtypeStruct((m, n_out), out_dtype),
        scratch_shapes=[pltpu.VMEM((tm, k), BF16)],
        compiler_params=_params(("arbitrary", "arbitrary"), vm),
        name=name,
    )(x, g.reshape(1, k), w3, *rope)


def _mm_resid_kernel(*refs, n_x):
    xs = refs[:n_x]
    ws = refs[n_x:2 * n_x]
    r_ref = refs[2 * n_x]
    o_ref = refs[2 * n_x + 1]
    acc = r_ref[...]
    for x_ref, w_ref in zip(xs, ws):
        acc = acc + _dot(x_ref[...].astype(BF16), w_ref[...].astype(BF16))
    o_ref[...] = acc


def matmul_resid(xs, w3, widx, resid, *, tm=512, tn=1024, name="mm_resid"):
    m, kx = xs[0].shape
    n = w3.shape[2]
    n_x = len(xs)
    assert w3.shape[1] == n_x * kx and m % tm == 0 and n % tn == 0
    in_specs = [pl.BlockSpec((tm, kx), lambda j, i: (i, 0)) for _ in xs]
    in_specs += [pl.BlockSpec((None, kx, tn), lambda j, i, q=q: (widx, q, j)) for q in range(n_x)]
    in_specs += [pl.BlockSpec((tm, tn), lambda j, i: (i, j))]
    vm = n_x * (2 * _nbytes((tm, kx), xs[0].dtype) + 3 * _nbytes((kx, tn), F32)) + 6 * _nbytes((tm, tn), F32) + (4 << 20)
    return pl.pallas_call(
        functools.partial(_mm_resid_kernel, n_x=n_x),
        grid=(n // tn, m // tm),
        in_specs=in_specs,
        out_specs=pl.BlockSpec((tm, tn), lambda j, i: (i, j)),
        out_shape=jax.ShapeDtypeStruct((m, n), F32),
        compiler_params=_params(("arbitrary", "arbitrary"), vm),
        name=name,
    )(*xs, *([w3] * n_x), resid)


def _rmsnorm_kernel(x_ref, g_ref, o_ref):
    o_ref[...] = _rms(x_ref[...], g_ref[...]).astype(o_ref.dtype)


def rmsnorm(x, g, out_dtype, *, tm=512, name="rmsnorm"):
    m, k = x.shape
    tm = min(tm, m)
    assert m % tm == 0
    vm = 4 * _nbytes((tm, k), F32) + 2 * _nbytes((tm, k), out_dtype) + (4 << 20)
    return pl.pallas_call(
        _rmsnorm_kernel,
        grid=(m // tm,),
        in_specs=[pl.BlockSpec((tm, k), lambda i: (i, 0)), pl.BlockSpec((1, k), lambda i: (0, 0))],
        out_specs=pl.BlockSpec((tm, k), lambda i: (i, 0)),
        out_shape=jax.ShapeDtypeStruct((m, k), out_dtype),
        compiler_params=_params(("arbitrary",), vm),
        name=name,
    )(x, g.reshape(1, k))


def _win_attn_kernel(sink_ref, q_ref, kp_ref, kc_ref, kn_ref, vp_ref, vc_ref, vn_ref, o_ref, *, tq, seq, grp):
    i = pl.program_id(1)
    kv = pl.program_id(2)
    hw = A_HALF_WINDOW
    k = jnp.concatenate([kp_ref[...], kc_ref[...], kn_ref[...]], axis=0)
    v = jnp.concatenate([vp_ref[...], vc_ref[...], vn_ref[...]], axis=0)
    wk = tq + 2 * hw
    qpos = i * tq + lax.broadcasted_iota(jnp.int32, (tq, wk), 0)
    kpos = i * tq - hw + lax.broadcasted_iota(jnp.int32, (tq, wk), 1)
    valid = (jnp.abs(kpos - qpos) <= hw) & (kpos >= 0) & (kpos < seq)
    for g in range(grp):
        sk = sink_ref[kv * grp + g]
        s = _dot_nt(q_ref[:, g * HEAD_DIM:(g + 1) * HEAD_DIM], k)
        s = jnp.where(valid, s, NEG)
        m = jnp.maximum(jnp.max(s, axis=1, keepdims=True), sk)
        p = jnp.exp(s - m)
        den = jnp.sum(p, axis=1, keepdims=True) + jnp.exp(sk - m)
        o = _dot(p.astype(BF16), v) / den
        o_ref[:, g * HEAD_DIM:(g + 1) * HEAD_DIM] = o.astype(o_ref.dtype)


def window_attention(z, sink, bsz, seq, *, tq=256, name="win_attn"):
    t = z.shape[0]
    hw = A_HALF_WINDOW
    grp = A_Q_HEADS // A_KV_HEADS
    nq = seq // tq
    r = tq // hw
    nhb = seq // hw
    qw = grp * HEAD_DIM
    kcol = COL_KA // HEAD_DIM
    vcol = COL_VA // HEAD_DIM
    cur = lambda col: pl.BlockSpec((tq, HEAD_DIM), lambda b, i, kv, s: (b * nq + i, col + kv))
    prev = lambda col: pl.BlockSpec((hw, HEAD_DIM), lambda b, i, kv, s: (b * nhb + jnp.maximum(i * r - 1, 0), col + kv))
    nxt = lambda col: pl.BlockSpec((hw, HEAD_DIM), lambda b, i, kv, s: (b * nhb + jnp.minimum((i + 1) * r, nhb - 1), col + kv))
    grid_spec = pltpu.PrefetchScalarGridSpec(
        num_scalar_prefetch=1,
        grid=(bsz, nq, A_KV_HEADS),
        in_specs=[pl.BlockSpec((tq, qw), lambda b, i, kv, s: (b * nq + i, kv)),
                  prev(kcol), cur(kcol), nxt(kcol), prev(vcol), cur(vcol), nxt(vcol)],
        out_specs=pl.BlockSpec((tq, qw), lambda b, i, kv, s: (b * nq + i, kv)),
    )
    return pl.pallas_call(
        functools.partial(_win_attn_kernel, tq=tq, seq=seq, grp=grp),
        grid_spec=grid_spec,
        out_shape=jax.ShapeDtypeStruct((t, A_Q_W), BF16),
        compiler_params=_params(("arbitrary",) * 3, 32 << 20),
        name=name,
    )(sink.astype(F32), z, z, z, z, z, z, z)


def _dilated_bias(tq, wk, reach):
    row = lax.broadcasted_iota(jnp.int32, (tq, wk), 0)
    col = lax.broadcasted_iota(jnp.int32, (tq, wk), 1)
    d = col - reach - row
    ad = jnp.abs(d)
    mult = jnp.zeros((tq, wk), jnp.int32)
    for window, dil in C_PATTERNS:
        mult = mult + ((ad <= window // 2) & ((d & (dil - 1)) == 0)).astype(jnp.int32)
    bias = jnp.where(mult == 1, 0.0, jnp.where(mult == 2, 1.0, math.log2(3.0)))
    return jnp.where(mult == 0, NEG, bias).astype(F32)


def _dil_attn_kernel(q_ref, k_ref, v_ref, o_ref, kp_ref, vp_ref, bias_ref, *, tq, seq, reach, nh, rb, nsub):
    b = pl.program_id(0)
    hg = pl.program_id(1)
    i = pl.program_id(2)
    wk = tq + 2 * reach
    hd = HEAD_DIM

    @pl.when((b == 0) & (hg == 0) & (i == 0))
    def _():
        bias_ref[...] = _dilated_bias(tq, wk, reach)
        kp_ref[...] = jnp.zeros_like(kp_ref)
        vp_ref[...] = jnp.zeros_like(vp_ref)
        lane = lax.broadcasted_iota(jnp.int32, (seq, hd), 1)
        ones_col = jnp.where(lane == 0, 1.0, 0.0).astype(BF16)
        for h in range(nh):
            vp_ref[h, reach:reach + seq, hd:2 * hd] = ones_col

    @pl.when(i == 0)
    def _():
        for h in range(nh):
            kp_ref[h, reach:reach + seq, :] = k_ref[:, h * hd:(h + 1) * hd]
            vp_ref[h, reach:reach + seq, 0:hd] = v_ref[:, h * hd:(h + 1) * hd]

    def attend(edge):
        for w in range(nsub):
            first = (i * nsub + w) * tq
            start = pl.multiple_of(first, tq)
            if edge:
                kpos = first - reach + lax.broadcasted_iota(jnp.int32, (1, wk), 1)
                colbias = jnp.where((kpos >= 0) & (kpos < seq), 0.0, NEG).astype(F32)
            for h in range(nh):
                ks = kp_ref[h, pl.ds(start, wk), :]
                vs = vp_ref[h, pl.ds(start, wk), :]
                for r0 in range(0, tq, rb):
                    q = q_ref[w * tq + r0:w * tq + r0 + rb, h * hd:(h + 1) * hd]
                    s = _dot_nt(q, ks) + bias_ref[r0:r0 + rb, :]
                    if edge:
                        s = s + colbias
                    m = jnp.max(s, axis=1, keepdims=True)
                    p = jnp.exp2(s - m).astype(BF16)
                    pv = _dot(p, vs)
                    o = pv[:, 0:hd] / pv[:, hd:hd + 1]
                    o_ref[w * tq + r0:w * tq + r0 + rb, h * hd:(h + 1) * hd] = o.astype(o_ref.dtype)

    is_edge = (i * nsub * tq < reach) | ((i + 1) * nsub * tq + reach > seq)
    pl.when(is_edge)(lambda: attend(True))
    pl.when(jnp.logical_not(is_edge))(lambda: attend(False))


def dilated_attention(z, bsz, seq, *, tq=256, nh=2, rb=128, nsub=4, name="dil_attn"):
    t = z.shape[0]
    reach = max(w // 2 for w, _ in C_PATTERNS)
    tb = nsub * tq
    nq = seq // tb
    wk = tq + 2 * reach
    ng = C_HEADS // nh
    gw = nh * HEAD_DIM
    assert seq % tb == 0 and seq >= wk and C_HEADS % nh == 0
    vm = 4 * _nbytes((seq, gw), BF16) + 3 * nh * _nbytes((seq + 2 * reach, HEAD_DIM), BF16) \
        + (1 + 3 * nh) * _nbytes((tq, wk), F32) + (8 << 20)
    return pl.pallas_call(
        functools.partial(_dil_attn_kernel, tq=tq, seq=seq, reach=reach, nh=nh, rb=rb, nsub=nsub),
        grid=(bsz, ng, nq),
        in_specs=[pl.BlockSpec((tb, gw), lambda b, g, i: (b * nq + i, g)),
                  pl.BlockSpec((seq, gw), lambda b, g, i: (b, ng + g)),
                  pl.BlockSpec((seq, gw), lambda b, g, i: (b, 2 * ng + g))],
        out_specs=pl.BlockSpec((tb, gw), lambda b, g, i: (b * nq + i, g)),
        out_shape=jax.ShapeDtypeStruct((t, D_MODEL), BF16),
        scratch_shapes=[pltpu.VMEM((nh, seq + 2 * reach, HEAD_DIM), BF16),
                        pltpu.VMEM((nh, seq + 2 * reach, 2 * HEAD_DIM), BF16),
                        pltpu.VMEM((tq, wk), F32)],
        compiler_params=_params(("arbitrary",) * 3, vm),
        name=name,
    )(z, z, z)


def _xattn_kernel(x_ref, g_ref, wq_ref, kv_ref, wo_ref, gn_ref, o_ref, *rest, emit_next):
    wq_s, wo_s = rest[-2:]

    @pl.when(pl.program_id(0) == 0)
    def _():
        wq_s[...] = wq_ref[...].astype(BF16)
        wo_s[...] = wo_ref[...].astype(BF16)

    x = x_ref[...]
    xn = _rms(x, g_ref[...]).astype(BF16)
    q = (_dot(xn, wq_s[...]) * (HEAD_DIM ** -0.5)).astype(BF16)
    kv = kv_ref[...]
    outs = []
    for h in range(X_HEADS):
        k = kv[:, h * HEAD_DIM:(h + 1) * HEAD_DIM]
        v = kv[:, X_W + h * HEAD_DIM:X_W + (h + 1) * HEAD_DIM]
        s = _dot_nt(q[:, h * HEAD_DIM:(h + 1) * HEAD_DIM], k)
        m = jnp.max(s, axis=1, keepdims=True)
        p = jnp.exp(s - m)
        den = jnp.sum(p, axis=1, keepdims=True)
        outs.append((_dot(p.astype(BF16), v) / den).astype(BF16))
    o = jnp.concatenate(outs, axis=1)
    y = x + _dot(o, wo_s[...])
    o_ref[...] = y
    if emit_next:
        rest[0][...] = _rms(y, gn_ref[...]).astype(BF16)


def cross_attention(x, g, wq3, wo3, layer, kv, seq, mem_len, next_gain=None, *, tm=512, name="xattn"):
    t, d = x.shape
    nsb = seq // tm
    assert seq % tm == 0
    emit_next = next_gain is not None
    vm = 4 * _nbytes((tm, d), F32) + 3 * _nbytes((d, X_W), F32) * 2 + 8 * _nbytes((tm, d), F32) + (8 << 20)
    row = pl.BlockSpec((tm, d), lambda i: (i, 0))
    vec = pl.BlockSpec((1, d), lambda i: (0, 0))
    out = pl.pallas_call(
        functools.partial(_xattn_kernel, emit_next=emit_next),
        grid=(t // tm,),
        in_specs=[row, vec,
                  pl.BlockSpec((None, d, X_W), lambda i: (layer, 0, 0)),
                  pl.BlockSpec((mem_len, 2 * X_W), lambda i: (i // nsb, 0)),
                  pl.BlockSpec((None, X_W, d), lambda i: (layer, 0, 0)),
                  vec],
        out_specs=[row, row] if emit_next else [row],
        out_shape=[jax.ShapeDtypeStruct((t, d), F32)] + ([jax.ShapeDtypeStruct((t, d), BF16)] if emit_next else []),
        scratch_shapes=[pltpu.VMEM((d, X_W), BF16), pltpu.VMEM((X_W, d), BF16)],
        compiler_params=_params(("arbitrary",), vm),
        name=name,
    )(x, g.reshape(1, d), wq3, kv, wo3, (next_gain if emit_next else g).reshape(1, d))
    return tuple(out) if emit_next else out[0]


def _conv_kernel(xp_ref, xc_ref, xn_ref, w_ref, o_ref, *, ts, ns, halo, q_scale, n_q_blocks):
    i = pl.program_id(1)
    j = pl.program_id(2)
    pad = B_CONV // 2
    xp = jnp.where(i > 0, xp_ref[halo - pad:, :].astype(F32), 0.0)
    xn = jnp.where(i < ns - 1, xn_ref[:pad, :].astype(F32), 0.0)
    xx = jnp.concatenate([xp, xc_ref[...].astype(F32), xn], axis=0)
    w = w_ref[...]
    acc = xx[0:ts, :] * w[0:1, :]
    for tap in range(1, B_CONV):
        acc = acc + xx[tap:tap + ts, :] * w[tap:tap + 1, :]
    y = acc * jax.nn.sigmoid(acc)
    y = y * jnp.where(j < n_q_blocks, q_scale, 1.0)
    o_ref[...] = y.astype(o_ref.dtype)


def conv_silu(z, conv_w8, bsz, seq, *, ts=512, tc=512, name="conv_silu"):
    t = z.shape[0]
    halo = 16
    ns = seq // ts
    nh = seq // halo
    cb = COL_QKB // tc
    assert COL_QKB % tc == 0 and seq % ts == 0 and B_W % tc == 0
    return pl.pallas_call(
        functools.partial(_conv_kernel, ts=ts, ns=ns, halo=halo, q_scale=B_HEAD_DIM ** -0.5, n_q_blocks=B_W // tc),
        grid=(bsz, ns, 2 * B_W // tc),
        in_specs=[pl.BlockSpec((halo, tc), lambda b, i, j: (b * nh + jnp.maximum(i * (ts // halo) - 1, 0), cb + j)),
                  pl.BlockSpec((ts, tc), lambda b, i, j: (b * ns + i, cb + j)),
                  pl.BlockSpec((halo, tc), lambda b, i, j: (b * nh + jnp.minimum((i + 1) * (ts // halo), nh - 1), cb + j)),
                  pl.BlockSpec((8, tc), lambda b, i, j: (0, j))],
        out_specs=pl.BlockSpec((ts, tc), lambda b, i, j: (b * ns + i, j)),
        out_shape=jax.ShapeDtypeStruct((t, 2 * B_W), BF16),
        compiler_params=_params(("arbitrary",) * 3, 32 << 20),
        name=name,
    )(z, z, z, conv_w8)


def _log_sigmoid(x):
    return jnp.minimum(x, 0.0) - jnp.log(1.0 + jnp.exp(-jnp.abs(x)))


def _mlstm_chunk(q, k, v, i_col, f_col, i_row, f_row, c_ref, n_ref, m_ref, rev):
    L = q.shape[0]
    logf_c = _log_sigmoid(f_col)
    logf_r = _log_sigmoid(f_row)
    row = lax.broadcasted_iota(jnp.int32, (L, L), 0)
    col = lax.broadcasted_iota(jnp.int32, (L, L), 1)
    causal = (col >= row) if rev else (col <= row)
    causal_t = (row >= col) if rev else (row <= col)
    b_col = jnp.sum(jnp.where(causal, logf_r, 0.0), axis=1, keepdims=True)
    b_row = jnp.sum(jnp.where(causal_t, logf_c, 0.0), axis=0, keepdims=True)
    b_all = jnp.sum(logf_r, axis=1, keepdims=True)
    m_st = m_ref[...]
    c_st = c_ref[...]
    n_st = n_ref[...]
    logd = jnp.where(causal, b_col - b_row + i_row, NEG)
    m_inter = b_col + m_st
    mt = jnp.maximum(m_inter, jnp.max(logd, axis=1, keepdims=True))
    s = _dot_nt(q, k) * jnp.exp(logd - mt)
    sc = jnp.exp(m_inter - mt)
    num = _dot(s.astype(BF16), v) + sc * _dot(q, c_st.astype(BF16))
    den = jnp.sum(s, axis=1, keepdims=True) + sc * jnp.sum(q.astype(F32) * n_st, axis=1, keepdims=True)
    h = num / jnp.maximum(jnp.abs(den), jnp.exp(-mt))
    logw = b_all - b_col + i_col
    m_new = jnp.maximum(b_all + m_st, jnp.max(logw, axis=0, keepdims=True))
    wgt = jnp.exp(logw - m_new)
    dec = jnp.exp(b_all + m_st - m_new)
    kw = k.astype(F32) * wgt
    c_ref[...] = dec * c_st + _dot_tn(kw.astype(BF16), v)
    n_ref[...] = dec * n_st + jnp.sum(kw, axis=0, keepdims=True)
    m_ref[...] = m_new
    return h


def _mlstm_kernel(bias_ref, qf_ref, kf_ref, vf0_ref, vf1_ref, gcf_ref, grf_ref,
                  qb_ref, kb_ref, vb0_ref, vb1_ref, gcb_ref, grb_ref,
                  hf_ref, hb_ref, c_ref, n_ref, m_ref):
    @pl.when(pl.program_id(1) == 0)
    def _():
        c_ref[...] = jnp.zeros_like(c_ref)
        n_ref[...] = jnp.zeros_like(n_ref)
        m_ref[...] = jnp.zeros_like(m_ref)

    lane = lax.broadcasted_iota(jnp.int32, (1, LANES), 1)
    dh = B_HEAD_DIM
    for d, (q_ref, k_ref, v0_ref, v1_ref, gc_ref, gr_ref, h_ref) in enumerate((
            (qf_ref, kf_ref, vf0_ref, vf1_ref, gcf_ref, grf_ref, hf_ref),
            (qb_ref, kb_ref, vb0_ref, vb1_ref, gcb_ref, grb_ref, hb_ref))):
        gc = gc_ref[...]
        for hd in range(B_HEADS):
            ci = 2 * d * B_HEADS + hd
            cf = ci + B_HEADS
            i_col = jnp.sum(jnp.where(lane == ci, gc, 0.0), axis=1, keepdims=True) + bias_ref[ci]
            f_col = jnp.sum(jnp.where(lane == cf, gc, 0.0), axis=1, keepdims=True) + bias_ref[cf]
            i_row = gr_ref[ci:ci + 1, :] + bias_ref[ci]
            f_row = gr_ref[cf:cf + 1, :] + bias_ref[cf]
            v_ref = v0_ref if hd < B_HEADS // 2 else v1_ref
            vo = (hd % (B_HEADS // 2)) * dh
            st = d * B_HEADS + hd
            h = _mlstm_chunk(q_ref[:, hd * dh:(hd + 1) * dh], k_ref[:, hd * dh:(hd + 1) * dh],
                             v_ref[:, vo:vo + dh], i_col, f_col, i_row, f_row,
                             c_ref.at[st], n_ref.at[st], m_ref.at[st], rev=bool(d))
            h_ref[:, hd * dh:(hd + 1) * dh] = h


def mlstm(qk, z, g_cols, g_rows, b_gate, bsz, seq, *, chunk=256, name="mlstm"):
    t = qk.shape[0]
    nc = seq // chunk
    assert seq % chunk == 0
    half = B_W // 2
    vcol = COL_VB // half
    assert COL_VB % half == 0
    fw = lambda b, c, s: b * nc + c
    bw = lambda b, c, s: b * nc + (nc - 1 - c)

    def specs(rowf):
        return [pl.BlockSpec((chunk, B_W), lambda b, c, s: (rowf(b, c, s), 0)),
                pl.BlockSpec((chunk, B_W), lambda b, c, s: (rowf(b, c, s), 1)),
                pl.BlockSpec((chunk, half), lambda b, c, s: (rowf(b, c, s), vcol)),
                pl.BlockSpec((chunk, half), lambda b, c, s: (rowf(b, c, s), vcol + 1)),
                pl.BlockSpec((chunk, LANES), lambda b, c, s: (rowf(b, c, s), 0)),
                pl.BlockSpec((16, chunk), lambda b, c, s: (0, rowf(b, c, s)))]

    grid_spec = pltpu.PrefetchScalarGridSpec(
        num_scalar_prefetch=1,
        grid=(bsz, nc),
        in_specs=specs(fw) + specs(bw),
        out_specs=[pl.BlockSpec((chunk, B_W), lambda b, c, s: (fw(b, c, s), 0)),
                   pl.BlockSpec((chunk, B_W), lambda b, c, s: (bw(b, c, s), 0))],
        scratch_shapes=[pltpu.VMEM((2 * B_HEADS, B_HEAD_DIM, B_HEAD_DIM), F32),
                        pltpu.VMEM((2 * B_HEADS, 1, B_HEAD_DIM), F32),
                        pltpu.VMEM((2 * B_HEADS, 1, 1), F32)],
    )
    args = (qk, qk, z, z, g_cols, g_rows)
    return pl.pallas_call(
        _mlstm_kernel,
        grid_spec=grid_spec,
        out_shape=[jax.ShapeDtypeStruct((t, B_W), F32), jax.ShapeDtypeStruct((t, B_W), F32)],
        compiler_params=_params(("arbitrary", "arbitrary"), 48 << 20),
        name=name,
    )(b_gate.astype(F32), *args, *args)


def _mlstm_out_kernel(hf_ref, hb_ref, o0_ref, o1_ref, gain_ref, y_ref):
    dh = B_HEAD_DIM
    for hd in range(B_HEADS):
        sl = slice(hd * dh, (hd + 1) * dh)
        h = hf_ref[:, sl] + hb_ref[:, sl]
        hc = h - jnp.mean(h, axis=1, keepdims=True)
        y = hc * lax.rsqrt(jnp.mean(hc * hc, axis=1, keepdims=True) + EPS) * gain_ref[:, sl]
        o_ref = o0_ref if hd < B_HEADS // 2 else o1_ref
        oo = (hd % (B_HEADS // 2)) * dh
        y_ref[:, sl] = (jax.nn.sigmoid(o_ref[:, oo:oo + dh].astype(F32)) * y).astype(y_ref.dtype)


def mlstm_output(hf, hb, z, gain, *, tm=512, name="mlstm_out"):
    t = hf.shape[0]
    half = B_W // 2
    ocol = COL_OB // half
    assert COL_OB % half == 0 and t % tm == 0
    return pl.pallas_call(
        _mlstm_out_kernel,
        grid=(t // tm,),
        in_specs=[pl.BlockSpec((tm, B_W), lambda i: (i, 0)),
                  pl.BlockSpec((tm, B_W), lambda i: (i, 0)),
                  pl.BlockSpec((tm, half), lambda i: (i, ocol)),
                  pl.BlockSpec((tm, half), lambda i: (i, ocol + 1)),
                  pl.BlockSpec((1, B_W), lambda i: (0, 0))],
        out_specs=pl.BlockSpec((tm, B_W), lambda i: (i, 0)),
        out_shape=jax.ShapeDtypeStruct((t, B_W), BF16),
        compiler_params=_params(("arbitrary",), 32 << 20),
        name=name,
    )(hf, hb, z, z, gain.reshape(1, B_W))


def _tile_state(na_ref, th_ref):
    i = pl.program_id(1)
    active = i < na_ref[0]
    half_only = th_ref[jnp.minimum(i, na_ref[0] - 1)] == 1
    return active, half_only


def _up_kernel(te_ref, na_ref, th_ref, x_ref, wg_ref, wu_ref, h_ref, *, tm):
    active, half_only = _tile_state(na_ref, th_ref)

    def compute(rows):
        x = x_ref[0:rows, :]
        g = _dot(x, wg_ref[...].astype(BF16))
        u = _dot(x, wu_ref[...].astype(BF16))
        h_ref[0:rows, :] = (g * jax.nn.sigmoid(g) * u).astype(h_ref.dtype)
        if rows < tm:
            h_ref[rows:, :] = jnp.zeros((tm - rows, h_ref.shape[1]), h_ref.dtype)

    pl.when(active & jnp.logical_not(half_only))(lambda: compute(tm))
    pl.when(active & half_only)(lambda: compute(tm // 2))

    @pl.when(jnp.logical_not(active))
    def _():
        h_ref[...] = jnp.zeros_like(h_ref)


def swiglu_up(x, w_gu3, tile_expert, n_active, tile_half, *, tm=512, tf=512, name="ffn_up"):
    p, d = x.shape
    f = w_gu3.shape[2] // 2
    nt = p // tm
    nj = f // tf
    assert p % tm == 0 and f % tf == 0
    row = lambda i, na: jnp.minimum(i, na[0] - 1)
    grid_spec = pltpu.PrefetchScalarGridSpec(
        num_scalar_prefetch=3,
        grid=(nj, nt),
        in_specs=[pl.BlockSpec((tm, d), lambda j, i, te, na, th: (row(i, na), 0)),
                  pl.BlockSpec((None, d, tf), lambda j, i, te, na, th: (te[row(i, na)], 0, j)),
                  pl.BlockSpec((None, d, tf), lambda j, i, te, na, th: (te[row(i, na)], 0, nj + j))],
        out_specs=pl.BlockSpec((tm, tf), lambda j, i, te, na, th: (i, j)),
    )
    vm = 2 * _nbytes((tm, d), BF16) + 2 * 3 * _nbytes((d, tf), F32) + 8 * _nbytes((tm, tf), F32) + (4 << 20)
    return pl.pallas_call(
        functools.partial(_up_kernel, tm=tm),
        grid_spec=grid_spec,
        out_shape=jax.ShapeDtypeStruct((p, f), BF16),
        compiler_params=_params(("arbitrary", "arbitrary"), vm),
        name=name,
    )(tile_expert, n_active, tile_half, x, w_gu3, w_gu3)


def _down_kernel(te_ref, na_ref, th_ref, h_ref, w_ref, *rest, tm, has_resid):
    o_ref = rest[-1]
    active, half_only = _tile_state(na_ref, th_ref)

    def compute(rows):
        o = _dot(h_ref[0:rows, :], w_ref[...].astype(BF16))
        if has_resid:
            o = o + rest[0][0:rows, :]
        o_ref[0:rows, :] = o
        if rows < tm:
            o_ref[rows:, :] = jnp.zeros((tm - rows, o_ref.shape[1]), o_ref.dtype)

    pl.when(active & jnp.logical_not(half_only))(lambda: compute(tm))
    pl.when(active & half_only)(lambda: compute(tm // 2))

    @pl.when(jnp.logical_not(active))
    def _():
        o_ref[...] = jnp.zeros_like(o_ref)


def swiglu_down(h, w_d3, tile_expert, n_active, tile_half, resid=None, *, tm=512, tn=512, name="ffn_down"):
    p, f = h.shape
    d = w_d3.shape[2]
    nt = p // tm
    assert p % tm == 0 and d % tn == 0
    row = lambda i, na: jnp.minimum(i, na[0] - 1)
    in_specs = [pl.BlockSpec((tm, f), lambda n, i, te, na, th: (row(i, na), 0)),
                pl.BlockSpec((None, f, tn), lambda n, i, te, na, th: (te[row(i, na)], 0, n))]
    args = [h, w_d3]
    if resid is not None:
        in_specs.append(pl.BlockSpec((tm, tn), lambda n, i, te, na, th: (row(i, na), n)))
        args.append(resid)
    grid_spec = pltpu.PrefetchScalarGridSpec(
        num_scalar_prefetch=3,
        grid=(d // tn, nt),
        in_specs=in_specs,
        out_specs=pl.BlockSpec((tm, tn), lambda n, i, te, na, th: (i, n)),
    )
    vm = 2 * _nbytes((tm, f), BF16) + 2 * _nbytes((f, tn), F32) + _nbytes((f, tn), BF16) + 8 * _nbytes((tm, tn), F32) + (4 << 20)
    return pl.pallas_call(
        functools.partial(_down_kernel, tm=tm, has_resid=resid is not None),
        grid_spec=grid_spec,
        out_shape=jax.ShapeDtypeStruct((p, d), F32),
        compiler_params=_params(("arbitrary", "arbitrary"), vm),
        name=name,
    )(tile_expert, n_active, tile_half, *args)


def _router_kernel(x_ref, g_ref, w_ref, mi_ref, mf_ref, cnt_ref, carry_ref, *, tm):
    @pl.when(pl.program_id(0) == 0)
    def _():
        carry_ref[...] = jnp.zeros_like(carry_ref)

    u = _rms(x_ref[...], g_ref[...])
    w = w_ref[...]
    u_hi = u.astype(BF16)
    w_hi = w.astype(BF16)
    u_lo = (u - u_hi.astype(F32)).astype(BF16)
    w_lo = (w - w_hi.astype(F32)).astype(BF16)
    logits = _dot(u_hi, w_hi) + (_dot(u_hi, w_lo) + _dot(u_lo, w_hi))
    lane = lax.broadcasted_iota(jnp.int32, (tm, LANES), 1)
    logits = jnp.where(lane < N_EXPERTS, logits, -jnp.inf)
    v1 = jnp.max(logits, axis=1, keepdims=True)
    i1 = jnp.min(jnp.where(logits == v1, lane, LANES), axis=1, keepdims=True)
    oh1 = lane == i1
    rest = jnp.where(oh1, -jnp.inf, logits)
    v2 = jnp.max(rest, axis=1, keepdims=True)
    i2 = jnp.min(jnp.where(rest == v2, lane, LANES), axis=1, keepdims=True)
    oh2 = lane == i2
    e = jnp.exp(v2 - v1)
    g1 = 1.0 / (1.0 + e)
    g2 = e * g1
    cnt = (oh1 | oh2).astype(F32)
    r = lax.broadcasted_iota(jnp.int32, (tm, tm), 0)
    c = lax.broadcasted_iota(jnp.int32, (tm, tm), 1)
    before = (c < r).astype(BF16)
    excl = _dot(before, cnt.astype(BF16)) + carry_ref[0:1, :]
    r1 = jnp.sum(jnp.where(oh1, excl, 0.0), axis=1, keepdims=True).astype(jnp.int32)
    r2 = jnp.sum(jnp.where(oh2, excl, 0.0), axis=1, keepdims=True).astype(jnp.int32)
    mi_ref[...] = jnp.where(lane == 0, i1, jnp.where(lane == 1, i2, jnp.where(lane == 2, r1, jnp.where(lane == 3, r2, 0))))
    mf_ref[...] = jnp.where(lane == 0, g1, jnp.where(lane == 1, g2, 0.0))
    carry_ref[...] = carry_ref[...] + jnp.sum(cnt, axis=0, keepdims=True)
    cnt_ref[...] = carry_ref[...]


def moe_router(x, g, w_router3, layer, *, tm=512, name="moe_router"):
    t, d = x.shape
    assert t % tm == 0
    vm = 6 * _nbytes((tm, d), F32) + 4 * _nbytes((d, LANES), F32) + 8 * _nbytes((tm, tm), F32) + (4 << 20)
    return pl.pallas_call(
        functools.partial(_router_kernel, tm=tm),
        grid=(t // tm,),
        in_specs=[pl.BlockSpec((tm, d), lambda i: (i, 0)),
                  pl.BlockSpec((1, d), lambda i: (0, 0)),
                  pl.BlockSpec((None, d, LANES), lambda i: (layer, 0, 0))],
        out_specs=[pl.BlockSpec((tm, LANES), lambda i: (i, 0)),
                   pl.BlockSpec((tm, LANES), lambda i: (i, 0)),
                   pl.BlockSpec((8, LANES), lambda i: (0, 0))],
        out_shape=[jax.ShapeDtypeStruct((t, LANES), jnp.int32),
                   jax.ShapeDtypeStruct((t, LANES), F32),
                   jax.ShapeDtypeStruct((8, LANES), F32)],
        scratch_shapes=[pltpu.VMEM((8, LANES), F32)],
        compiler_params=_params(("arbitrary",), vm),
        name=name,
    )(x, g.reshape(1, d), w_router3)


def _row_copy(src_hbm, dst_vmem, sem, src_row, dst_row):
    return pltpu.make_async_copy(src_hbm.at[pl.ds(src_row, 1), :], dst_vmem.at[pl.ds(dst_row, 1), :], sem)


GATHER_UNROLL = 8


def _start_rows(src_hbm, idx_refs, dst_refs, sem, n):
    for r in range(n):
        for idx_ref, dst in zip(idx_refs, dst_refs):
            _row_copy(src_hbm, dst, sem, idx_ref[0, 0, r], r).start(priority=r % 2)


def _wait_rows(src_hbm, dst_refs, sem, n):
    def body(r, carry):
        for dst in dst_refs:
            _row_copy(src_hbm, dst, sem, 0, r).wait()
        return carry

    lax.fori_loop(0, n, body, 0, unroll=GATHER_UNROLL)


def _dispatch_kernel(tok_ref, tok_next_ref, x_hbm, g_ref, o_ref, rows_ref, sems, *, tg):
    i = pl.program_id(0)
    slot = i % 2

    @pl.when(i == 0)
    def _():
        _start_rows(x_hbm, [tok_ref], [rows_ref.at[0]], sems.at[0], tg)

    @pl.when(i + 1 < pl.num_programs(0))
    def _():
        _start_rows(x_hbm, [tok_next_ref], [rows_ref.at[1 - slot]], sems.at[1 - slot], tg)

    _wait_rows(x_hbm, [rows_ref.at[slot]], sems.at[slot], tg)
    o_ref[...] = _rms(rows_ref[slot], g_ref[...]).astype(o_ref.dtype)


def moe_dispatch(x, g, token_of, *, tg=256, name="moe_dispatch"):
    t, d = x.shape
    p = token_of.shape[0]
    assert p % tg == 0
    nt = p // tg
    vm = 4 * _nbytes((tg, d), F32) + 2 * _nbytes((tg, d), BF16) + (4 << 20)
    tok3 = token_of.reshape(nt, 1, tg)
    return pl.pallas_call(
        functools.partial(_dispatch_kernel, tg=tg),
        grid=(nt,),
        in_specs=[pl.BlockSpec((1, 1, tg), lambda i: (i, 0, 0), memory_space=pltpu.SMEM),
                  pl.BlockSpec((1, 1, tg), lambda i: (jnp.minimum(i + 1, nt - 1), 0, 0), memory_space=pltpu.SMEM),
                  pl.BlockSpec(memory_space=pl.ANY),
                  pl.BlockSpec((1, d), lambda i: (0, 0))],
        out_specs=pl.BlockSpec((tg, d), lambda i: (i, 0)),
        out_shape=jax.ShapeDtypeStruct((p, d), BF16),
        scratch_shapes=[pltpu.VMEM((2, tg, d), F32), pltpu.SemaphoreType.DMA((2,))],
        compiler_params=_params(("arbitrary",), vm),
        name=name,
    )(tok3, tok3, x, g.reshape(1, d))


def _combine_kernel(p1_ref, p2_ref, p1n_ref, p2n_ref, o_hbm, mf_ref, x_ref, gf_ref, y_ref, rows_ref, sems, *, tc,
                    final_norm):
    i = pl.program_id(0)
    slot = i % 2
    dst = lambda s: [rows_ref.at[s, 0], rows_ref.at[s, 1]]

    @pl.when(i == 0)
    def _():
        _start_rows(o_hbm, [p1_ref, p2_ref], dst(0), sems.at[0], tc)

    @pl.when(i + 1 < pl.num_programs(0))
    def _():
        _start_rows(o_hbm, [p1n_ref, p2n_ref], dst(1 - slot), sems.at[1 - slot], tc)

    _wait_rows(o_hbm, dst(slot), sems.at[slot], tc)
    mf = mf_ref[...]
    y = x_ref[...] + mf[:, 0:1] * rows_ref[slot, 0] + mf[:, 1:2] * rows_ref[slot, 1]
    y_ref[...] = _rms(y, gf_ref[...]) if final_norm else y


def moe_combine(o, pos1, pos2, gates, x, final_gain=None, *, tc=256, name="moe_combine"):
    t, d = x.shape
    final_norm = final_gain is not None
    assert t % tc == 0
    nt = t // tc
    vm = 4 * _nbytes((tc, d), F32) + 6 * _nbytes((tc, d), F32) + (4 << 20)
    cur = pl.BlockSpec((1, 1, tc), lambda i: (i, 0, 0), memory_space=pltpu.SMEM)
    nxt = pl.BlockSpec((1, 1, tc), lambda i: (jnp.minimum(i + 1, nt - 1), 0, 0), memory_space=pltpu.SMEM)
    p1 = pos1.reshape(nt, 1, tc)
    p2 = pos2.reshape(nt, 1, tc)
    return pl.pallas_call(
        functools.partial(_combine_kernel, tc=tc, final_norm=final_norm),
        grid=(nt,),
        in_specs=[cur, cur, nxt, nxt,
                  pl.BlockSpec(memory_space=pl.ANY),
                  pl.BlockSpec((tc, LANES), lambda i: (i, 0)),
                  pl.BlockSpec((tc, d), lambda i: (i, 0)),
                  pl.BlockSpec((1, d), lambda i: (0, 0))],
        out_specs=pl.BlockSpec((tc, d), lambda i: (i, 0)),
        out_shape=jax.ShapeDtypeStruct((t, d), F32),
        scratch_shapes=[pltpu.VMEM((2, 2, tc, d), F32), pltpu.SemaphoreType.DMA((2,))],
        compiler_params=_params(("arbitrary",), vm),
        name=name,
    )(p1, p2, p1, p2, o, gates, x, (final_gain if final_norm else jnp.ones((d,), F32)).reshape(1, d))


def _rope_tables(seq):
    pos = jnp.arange(seq, dtype=F32)
    inv = jnp.power(ROPE_THETA, -jnp.arange(0, ROT_DIM, 2, dtype=F32) / ROT_DIM)
    ang = pos[:, None] * inv[None, :]
    cos, sin = jnp.cos(ang), jnp.sin(ang)
    half = ROT_DIM // 2
    zeros = jnp.zeros((seq, LANES - ROT_DIM), F32)
    c = jnp.concatenate([cos, cos, jnp.ones_like(zeros)], axis=1)
    s1 = jnp.concatenate([-sin, jnp.zeros((seq, half), F32), zeros], axis=1)
    s2 = jnp.concatenate([jnp.zeros((seq, half), F32), sin, zeros], axis=1)
    return c, s1, s2


def _moe_plan(meta_i, counts, t, tm):
    eid = meta_i[:, 0:2]
    rank = meta_i[:, 2:4]
    cnt = counts[0, :N_EXPERTS].astype(jnp.int32)
    padded = ((cnt + tm - 1) // tm) * tm
    ends = jnp.cumsum(padded)
    starts = ends - padded
    pos = starts[eid] + rank
    p_max = 2 * t + N_EXPERTS * tm
    nt = p_max // tm
    n_active = jnp.maximum(ends[-1] // tm, 1)
    tile_start = jnp.minimum(jnp.arange(nt, dtype=jnp.int32), n_active - 1) * tm
    tile_expert = jnp.minimum(jnp.searchsorted(ends, tile_start, side="right"), N_EXPERTS - 1).astype(jnp.int32)
    tok = jnp.broadcast_to(jnp.arange(t, dtype=jnp.int32)[:, None], (t, 2))
    token_of = jnp.zeros((p_max,), jnp.int32).at[pos.reshape(-1)].set(tok.reshape(-1))
    valid = jnp.clip((starts + cnt)[tile_expert] - tile_start, 0, tm)
    tile_half = (valid <= tm // 2).astype(jnp.int32)
    return pos[:, 0], pos[:, 1], token_of, tile_expert, n_active.reshape(1).astype(jnp.int32), tile_half


FFN_TILE = 512
MOE_UP_COLS = 1024
DENSE_UP_TILE = 1024


def even_mixer(xs, g, w_in3, gate_w3, conv_w8, b_gate, sink, head_gain, w_out3, j, rope, bsz, seq):
    z = norm_matmul(xs, g, w_in3, j, EVEN_MAIN, rope=rope, seq=seq, n_rope=(A_Q_W + A_KV_W) // HEAD_DIM,
                    n_scale=A_Q_W // HEAD_DIM, scale=HEAD_DIM ** -0.5, name="even_in")
    g_cols = norm_matmul(xs, g, gate_w3, j, LANES, out_dtype=F32, name="even_gates")
    g_rows = g_cols[:, :4 * B_HEADS].T
    ya = window_attention(z, sink, bsz, seq)
    qk = conv_silu(z, conv_w8, bsz, seq)
    hf, hb = mlstm(qk, z, g_cols, g_rows, b_gate, bsz, seq)
    yb = mlstm_output(hf, hb, z, head_gain)
    return matmul_resid([ya, yb], w_out3, j, xs, name="even_out")


def odd_mixer(xs, g, w_in3, w_out3, j, rope, bsz, seq):
    d = xs.shape[1]
    z = norm_matmul(xs, g, w_in3, j, 3 * d, rope=rope, seq=seq, n_rope=2 * C_HEADS, n_scale=C_HEADS,
                    scale=HEAD_DIM ** -0.5 * math.log2(math.e), name="odd_in")
    y = dilated_attention(z, bsz, seq)
    return matmul_resid([y], w_out3, j, xs, name="odd_out")


def dense_ffn(xs, u, w_gu3, w_d3, j):
    t = xs.shape[0]
    plan = lambda tm: (jnp.full((t // tm,), j, jnp.int32), jnp.full((1,), t // tm, jnp.int32),
                       jnp.zeros((t // tm,), jnp.int32))
    h = swiglu_up(u, w_gu3, *plan(DENSE_UP_TILE), tm=DENSE_UP_TILE, name="ffn_up")
    return swiglu_down(h, w_d3, *plan(FFN_TILE), resid=xs, tm=FFN_TILE, name="ffn_down")


def moe_ffn(xs, g, router_w3, w_gu3, w_d3, j, final_gain=None):
    t = xs.shape[0]
    meta_i, gates, counts = moe_router(xs, g, router_w3, j)
    pos1, pos2, token_of, tile_expert, n_active, tile_half = _moe_plan(meta_i, counts, t, FFN_TILE)
    xd = moe_dispatch(xs, g, token_of)
    experts = tile_expert + j * N_EXPERTS
    h = swiglu_up(xd, w_gu3, experts, n_active, tile_half, tm=FFN_TILE, tf=MOE_UP_COLS, name="moe_up")
    o = swiglu_down(h, w_d3, experts, n_active, tile_half, tm=FFN_TILE, name="moe_down")
    return moe_combine(o, pos1, pos2, gates, xs, final_gain)


def kernel(x, mem, ln_mix, ln_xattn, ln_mem, ln_ffn, ln_final, ev_w_in, ev_b_gate, ev_conv, ev_sink, ev_head_norm,
           ev_w_out, ffn_w_gu, ffn_w_down, od_w_in, od_w_out, moe_router, moe_w_gu, moe_w_down, x_wq, x_wkv, x_wo):
    bsz, seq, d = x.shape
    mem_len = mem.shape[1]
    depth = ln_mix.shape[0]
    t = bsz * seq
    rope = _rope_tables(seq)
    xs = x.reshape(t, d)
    memf = mem.reshape(bsz * mem_len, d)
    n_odd = od_w_in.shape[0]
    moe_gu = moe_w_gu.reshape(n_odd * N_EXPERTS, d, 2 * FFN_DIM)
    moe_dn = moe_w_down.reshape(n_odd * N_EXPERTS, FFN_DIM, d)
    router_w = jnp.pad(moe_router, ((0, 0), (0, 0), (0, LANES - N_EXPERTS)))
    gate_w = jnp.pad(ev_w_in[:, :, EVEN_MAIN:], ((0, 0), (0, 0), (0, LANES - 4 * B_HEADS)))
    conv_w8 = jnp.pad(ev_conv, ((0, 0), (0, 8 - B_CONV), (0, 0)))

    for layer in range(depth):
        j = layer // 2
        if layer % 2 == 0:
            xs = even_mixer(xs, ln_mix[layer], ev_w_in, gate_w, conv_w8[j], ev_b_gate[j], ev_sink[j],
                            ev_head_norm[j], ev_w_out, j, rope, bsz, seq)
        else:
            xs = odd_mixer(xs, ln_mix[layer], od_w_in, od_w_out, j, rope, bsz, seq)
        kv = norm_matmul(memf, ln_mem[layer], x_wkv, layer, 2 * X_W, name="mem_kv")
        last = layer == depth - 1
        if layer % 2 == 0:
            xs, u = cross_attention(xs, ln_xattn[layer], x_wq, x_wo, layer, kv, seq, mem_len, ln_ffn[layer])
            xs = dense_ffn(xs, u, ffn_w_gu, ffn_w_down, j)
            if last:
                xs = rmsnorm(xs, ln_final, F32, name="final_norm")
        else:
            xs = cross_attention(xs, ln_xattn[layer], x_wq, x_wo, layer, kv, seq, mem_len)
            xs = moe_ffn(xs, ln_ffn[layer], router_w, moe_gu, moe_dn, j, ln_final if last else None)
    return xs.reshape(bsz, seq, d)
```

```python
import functools
import math

import jax
import jax.numpy as jnp
from jax import lax
from jax.experimental import pallas as pl
from jax.experimental.pallas import tpu as pltpu

D_MODEL = 2048
HEAD_DIM = 128
A_Q_HEADS = 8
A_KV_HEADS = 2
A_HALF_WINDOW = 128
B_HEADS = 4
B_HEAD_DIM = 256
B_CONV = 5
C_HEADS = D_MODEL // HEAD_DIM
C_PATTERNS = ((128, 1), (512, 4), (2048, 16))
X_HEADS = 4
FFN_DIM = 7168
N_EXPERTS = 8
ROPE_THETA = 500000.0
ROT_DIM = HEAD_DIM // 4
EPS = 1e-6

A_Q_W = A_Q_HEADS * HEAD_DIM
A_KV_W = A_KV_HEADS * HEAD_DIM
B_W = B_HEADS * B_HEAD_DIM
X_W = X_HEADS * HEAD_DIM
EVEN_MAIN = A_Q_W + 2 * A_KV_W + 4 * B_W
COL_KA = A_Q_W
COL_VA = A_Q_W + A_KV_W
COL_QKB = A_Q_W + 2 * A_KV_W
COL_VB = COL_QKB + 2 * B_W
COL_OB = COL_VB + B_W

LANES = 128
V7X_VMEM_BYTES = 64 * 1024 * 1024
VMEM_CAP = V7X_VMEM_BYTES - 8 * 1024 * 1024

NEG = -1e30
BF16 = jnp.bfloat16
F32 = jnp.float32


def _params(sem, vmem_bytes):
    return pltpu.CompilerParams(dimension_semantics=sem, vmem_limit_bytes=int(min(VMEM_CAP, vmem_bytes)))


def _nbytes(shape, dtype):
    return math.prod(shape) * jnp.dtype(dtype).itemsize


def _dot(a, b):
    return jnp.dot(a, b, preferred_element_type=F32)


def _dot_nt(a, b):
    return lax.dot_general(a, b, (((1,), (1,)), ((), ())), preferred_element_type=F32)


def _dot_tn(a, b):
    return lax.dot_general(a, b, (((0,), (0,)), ((), ())), preferred_element_type=F32)


def _rms(x, g):
    return x * lax.rsqrt(jnp.mean(x * x, axis=-1, keepdims=True) + EPS) * g


def _rope_tile(z, c, s1, s2):
    return z * c + pltpu.roll(z, LANES - ROT_DIM // 2, 1) * s1 + pltpu.roll(z, ROT_DIM // 2, 1) * s2


def _norm_mm_kernel(x_ref, g_ref, w_ref, c_ref, s1_ref, s2_ref, o_ref, xn_ref, *, tn, n_steps, n_rope, n_scale,
                    scale):
    j = pl.program_id(1)

    @pl.when(j == 0)
    def _():
        xn_ref[...] = _rms(x_ref[...], g_ref[...]).astype(BF16)

    heads = tn // LANES
    half = min(tn, 2 * LANES)

    def step(kinds):
        xn = xn_ref[...]
        for c0 in range(0, tn, half):
            z = _dot(xn, w_ref[:, c0:c0 + half].astype(BF16))
            for hh in range(c0 // LANES, (c0 + half) // LANES):
                zt = z[:, hh * LANES - c0:(hh + 1) * LANES - c0]
                rotate, scaled = kinds[hh]
                if rotate:
                    zt = _rope_tile(zt, c_ref[...], s1_ref[...], s2_ref[...])
                if scaled:
                    zt = zt * scale
                o_ref[:, hh * LANES:(hh + 1) * LANES] = zt.astype(o_ref.dtype)

    kinds_of = lambda jj: tuple((jj * heads + hh < n_rope, jj * heads + hh < n_scale) for hh in range(heads))
    groups = {}
    for jj in range(n_steps):
        groups.setdefault(kinds_of(jj), []).append(jj)
    for kinds, js in groups.items():
        assert js == list(range(js[0], js[-1] + 1))
        if len(groups) == 1:
            step(kinds)
        else:
            pl.when((j >= js[0]) & (j <= js[-1]))(functools.partial(step, kinds))


def norm_matmul(x, g, w3, widx, n_out, *, rope=None, n_rope=0, n_scale=0, scale=1.0, seq=None, tm=1024, tn=512,
                out_dtype=BF16, name="norm_mm"):
    m, k = x.shape
    tm = min(tm, m)
    tn = min(tn, n_out)
    assert m % tm == 0 and n_out % tn == 0 and tn % LANES == 0
    if rope is None:
        dummy = jnp.zeros((8, LANES), F32)
        rope = (dummy, dummy, dummy)
        rspec = pl.BlockSpec((8, LANES), lambda i, j: (0, 0))
    else:
        nsb = seq // tm
        assert seq % tm == 0
        rspec = pl.BlockSpec((tm, LANES), lambda i, j: (i % nsb, 0))
    vm = 2 * _nbytes((tm, k), F32) + _nbytes((tm, k), BF16) + 2 * _nbytes((k, tn), F32) + _nbytes((k, tn), BF16) \
        + 2 * _nbytes((tm, tn), out_dtype) + 2 * _nbytes((tm, tn), F32) + 6 * _nbytes((tm, LANES), F32) + (4 << 20)
    return pl.pallas_call(
        functools.partial(_norm_mm_kernel, tn=tn, n_steps=n_out // tn, n_rope=n_rope, n_scale=n_scale, scale=scale),
        grid=(m // tm, n_out // tn),
        in_specs=[pl.BlockSpec((tm, k), lambda i, j: (i, 0)),
                  pl.BlockSpec((1, k), lambda i, j: (0, 0)),
                  pl.BlockSpec((None, k, tn), lambda i, j: (widx, 0, j)),
                  rspec, rspec, rspec],
        out_specs=pl.BlockSpec((tm, tn), lambda i, j: (i, j)),
        out_shape=jax.ShapeDtypeStruct((m, n_out), out_dtype),
        scratch_shapes=[pltpu.VMEM((tm, k), BF16)],
        compiler_params=_params(("arbitrary", "arbitrary"), vm),
        name=name,
    )(x, g.reshape(1, k), w3, *rope)


def _mm_resid_kernel(*refs, n_x):
    xs = refs[:n_x]
    ws = refs[n_x:2 * n_x]
    r_ref = refs[2 * n_x]
    o_ref = refs[2 * n_x + 1]
    acc = r_ref[...]
    for x_ref, w_ref in zip(xs, ws):
        acc = acc + _dot(x_ref[...].astype(BF16), w_ref[...].astype(BF16))
    o_ref[...] = acc


def matmul_resid(xs, w3, widx, resid, *, tm=1024, tn=1024, name="mm_resid"):
    m, kx = xs[0].shape
    n = w3.shape[2]
    n_x = len(xs)
    assert w3.shape[1] == n_x * kx and m % tm == 0 and n % tn == 0
    in_specs = [pl.BlockSpec((tm, kx), lambda j, i: (i, 0)) for _ in xs]
    in_specs += [pl.BlockSpec((None, kx, tn), lambda j, i, q=q: (widx, q, j)) for q in range(n_x)]
    in_specs += [pl.BlockSpec((tm, tn), lambda j, i: (i, j))]
    vm = n_x * (2 * _nbytes((tm, kx), xs[0].dtype) + 3 * _nbytes((kx, tn), F32)) + 6 * _nbytes((tm, tn), F32) + (4 << 20)
    return pl.pallas_call(
        functools.partial(_mm_resid_kernel, n_x=n_x),
        grid=(n // tn, m // tm),
        in_specs=in_specs,
        out_specs=pl.BlockSpec((tm, tn), lambda j, i: (i, j)),
        out_shape=jax.ShapeDtypeStruct((m, n), F32),
        compiler_params=_params(("arbitrary", "arbitrary"), vm),
        name=name,
    )(*xs, *([w3] * n_x), resid)


def _rmsnorm_kernel(x_ref, g_ref, o_ref):
    o_ref[...] = _rms(x_ref[...], g_ref[...]).astype(o_ref.dtype)


def rmsnorm(x, g, out_dtype, *, tm=512, name="rmsnorm"):
    m, k = x.shape
    tm = min(tm, m)
    assert m % tm == 0
    vm = 4 * _nbytes((tm, k), F32) + 2 * _nbytes((tm, k), out_dtype) + (4 << 20)
    return pl.pallas_call(
        _rmsnorm_kernel,
        grid=(m // tm,),
        in_specs=[pl.BlockSpec((tm, k), lambda i: (i, 0)), pl.BlockSpec((1, k), lambda i: (0, 0))],
        out_specs=pl.BlockSpec((tm, k), lambda i: (i, 0)),
        out_shape=jax.ShapeDtypeStruct((m, k), out_dtype),
        compiler_params=_params(("arbitrary",), vm),
        name=name,
    )(x, g.reshape(1, k))


def _win_attn_kernel(sink_ref, q_ref, kp_ref, kc_ref, kn_ref, vp_ref, vc_ref, vn_ref, o_ref, *, tq, seq, grp):
    i = pl.program_id(1)
    hw = A_HALF_WINDOW
    hd = HEAD_DIM
    k = jnp.concatenate([kp_ref[...], kc_ref[...], kn_ref[...]], axis=0)
    v = jnp.concatenate([vp_ref[...], vc_ref[...], vn_ref[...]], axis=0)
    wk = tq + 2 * hw
    qpos = i * tq + lax.broadcasted_iota(jnp.int32, (tq, wk), 0)
    kpos = i * tq - hw + lax.broadcasted_iota(jnp.int32, (tq, wk), 1)
    valid = (jnp.abs(kpos - qpos) <= hw) & (kpos >= 0) & (kpos < seq)
    for kv in range(A_KV_HEADS):
        kh = k[:, kv * hd:(kv + 1) * hd]
        vh = v[:, kv * hd:(kv + 1) * hd]
        for g in range(grp):
            head = kv * grp + g
            sk = sink_ref[head]
            s = _dot_nt(q_ref[:, head * hd:(head + 1) * hd], kh)
            s = jnp.where(valid, s, NEG)
            m = jnp.maximum(jnp.max(s, axis=1, keepdims=True), sk)
            p = jnp.exp(s - m)
            den = jnp.sum(p, axis=1, keepdims=True) + jnp.exp(sk - m)
            o = _dot(p.astype(BF16), vh) / den
            o_ref[:, head * hd:(head + 1) * hd] = o.astype(o_ref.dtype)


def window_attention(z, sink, bsz, seq, *, tq=256, name="win_attn"):
    t = z.shape[0]
    hw = A_HALF_WINDOW
    grp = A_Q_HEADS // A_KV_HEADS
    nq = seq // tq
    r = tq // hw
    nhb = seq // hw
    kcol = COL_KA // A_KV_W
    vcol = COL_VA // A_KV_W
    assert COL_KA % A_KV_W == 0 and COL_VA % A_KV_W == 0
    cur = lambda col: pl.BlockSpec((tq, A_KV_W), lambda b, i, s: (b * nq + i, col))
    prev = lambda col: pl.BlockSpec((hw, A_KV_W), lambda b, i, s: (b * nhb + jnp.maximum(i * r - 1, 0), col))
    nxt = lambda col: pl.BlockSpec((hw, A_KV_W), lambda b, i, s: (b * nhb + jnp.minimum((i + 1) * r, nhb - 1), col))
    grid_spec = pltpu.PrefetchScalarGridSpec(
        num_scalar_prefetch=1,
        grid=(bsz, nq),
        in_specs=[pl.BlockSpec((tq, A_Q_W), lambda b, i, s: (b * nq + i, 0)),
                  prev(kcol), cur(kcol), nxt(kcol), prev(vcol), cur(vcol), nxt(vcol)],
        out_specs=pl.BlockSpec((tq, A_Q_W), lambda b, i, s: (b * nq + i, 0)),
    )
    return pl.pallas_call(
        functools.partial(_win_attn_kernel, tq=tq, seq=seq, grp=grp),
        grid_spec=grid_spec,
        out_shape=jax.ShapeDtypeStruct((t, A_Q_W), BF16),
        compiler_params=_params(("arbitrary",) * 2, 32 << 20),
        name=name,
    )(sink.astype(F32), z, z, z, z, z, z, z)


def _dilated_bias(tq, wk, reach):
    row = lax.broadcasted_iota(jnp.int32, (tq, wk), 0)
    col = lax.broadcasted_iota(jnp.int32, (tq, wk), 1)
    d = col - reach - row
    ad = jnp.abs(d)
    mult = jnp.zeros((tq, wk), jnp.int32)
    for window, dil in C_PATTERNS:
        mult = mult + ((ad <= window // 2) & ((d & (dil - 1)) == 0)).astype(jnp.int32)
    bias = jnp.where(mult == 1, 0.0, jnp.where(mult == 2, 1.0, math.log2(3.0)))
    return jnp.where(mult == 0, NEG, bias).astype(F32)


def _dil_attn_kernel(q_ref, k_ref, v_ref, o_ref, kp_ref, vp_ref, bias_ref, *, tq, seq, reach, nh, rb, nsub):
    b = pl.program_id(0)
    hg = pl.program_id(1)
    i = pl.program_id(2)
    wk = tq + 2 * reach
    hd = HEAD_DIM

    @pl.when((b == 0) & (hg == 0) & (i == 0))
    def _():
        bias_ref[...] = _dilated_bias(tq, wk, reach)
        kp_ref[...] = jnp.zeros_like(kp_ref)
        vp_ref[...] = jnp.zeros_like(vp_ref)
        lane = lax.broadcasted_iota(jnp.int32, (seq, hd), 1)
        ones_col = jnp.where(lane == 0, 1.0, 0.0).astype(BF16)
        for h in range(nh):
            vp_ref[h, reach:reach + seq, hd:2 * hd] = ones_col

    @pl.when(i == 0)
    def _():
        for h in range(nh):
            kp_ref[h, reach:reach + seq, :] = k_ref[:, h * hd:(h + 1) * hd]
            vp_ref[h, reach:reach + seq, 0:hd] = v_ref[:, h * hd:(h + 1) * hd]

    def attend(edge):
        for w in range(nsub):
            first = (i * nsub + w) * tq
            start = pl.multiple_of(first, tq)
            if edge:
                kpos = first - reach + lax.broadcasted_iota(jnp.int32, (1, wk), 1)
                colbias = jnp.where((kpos >= 0) & (kpos < seq), 0.0, NEG).astype(F32)
            for h in range(nh):
                ks = kp_ref[h, pl.ds(start, wk), :]
                vs = vp_ref[h, pl.ds(start, wk), :]
                for r0 in range(0, tq, rb):
                    q = q_ref[w * tq + r0:w * tq + r0 + rb, h * hd:(h + 1) * hd]
                    s = _dot_nt(q, ks) + bias_ref[r0:r0 + rb, :]
                    if edge:
                        s = s + colbias
                    m = jnp.max(s, axis=1, keepdims=True)
                    p = jnp.exp2(s - m).astype(BF16)
                    pv = _dot(p, vs)
                    o = pv[:, 0:hd] / pv[:, hd:hd + 1]
                    o_ref[w * tq + r0:w * tq + r0 + rb, h * hd:(h + 1) * hd] = o.astype(o_ref.dtype)

    is_edge = (i * nsub * tq < reach) | ((i + 1) * nsub * tq + reach > seq)
    pl.when(is_edge)(lambda: attend(True))
    pl.when(jnp.logical_not(is_edge))(lambda: attend(False))


def dilated_attention(z, bsz, seq, *, tq=256, nh=2, rb=128, nsub=4, name="dil_attn"):
    t = z.shape[0]
    reach = max(w // 2 for w, _ in C_PATTERNS)
    tb = nsub * tq
    nq = seq // tb
    wk = tq + 2 * reach
    ng = C_HEADS // nh
    gw = nh * HEAD_DIM
    assert seq % tb == 0 and seq >= wk and C_HEADS % nh == 0
    vm = 4 * _nbytes((seq, gw), BF16) + 3 * nh * _nbytes((seq + 2 * reach, HEAD_DIM), BF16) \
        + (1 + 3 * nh) * _nbytes((tq, wk), F32) + (8 << 20)
    return pl.pallas_call(
        functools.partial(_dil_attn_kernel, tq=tq, seq=seq, reach=reach, nh=nh, rb=rb, nsub=nsub),
        grid=(bsz, ng, nq),
        in_specs=[pl.BlockSpec((tb, gw), lambda b, g, i: (b * nq + i, g)),
                  pl.BlockSpec((seq, gw), lambda b, g, i: (b, ng + g)),
                  pl.BlockSpec((seq, gw), lambda b, g, i: (b, 2 * ng + g))],
        out_specs=pl.BlockSpec((tb, gw), lambda b, g, i: (b * nq + i, g)),
        out_shape=jax.ShapeDtypeStruct((t, D_MODEL), BF16),
        scratch_shapes=[pltpu.VMEM((nh, seq + 2 * reach, HEAD_DIM), BF16),
                        pltpu.VMEM((nh, seq + 2 * reach, 2 * HEAD_DIM), BF16),
                        pltpu.VMEM((tq, wk), F32)],
        compiler_params=_params(("arbitrary",) * 3, vm),
        name=name,
    )(z, z, z)


def _xattn_kernel(x_ref, g_ref, wq_ref, kv_ref, wo_ref, gn_ref, o_ref, *rest, emit_next):
    wq_s, wo_s = rest[-2:]

    @pl.when(pl.program_id(0) == 0)
    def _():
        wq_s[...] = wq_ref[...].astype(BF16)
        wo_s[...] = wo_ref[...].astype(BF16)

    x = x_ref[...]
    xn = _rms(x, g_ref[...]).astype(BF16)
    q = (_dot(xn, wq_s[...]) * (HEAD_DIM ** -0.5)).astype(BF16)
    kv = kv_ref[...]
    outs = []
    for h in range(X_HEADS):
        k = kv[:, h * HEAD_DIM:(h + 1) * HEAD_DIM]
        v = kv[:, X_W + h * HEAD_DIM:X_W + (h + 1) * HEAD_DIM]
        s = _dot_nt(q[:, h * HEAD_DIM:(h + 1) * HEAD_DIM], k)
        m = jnp.max(s, axis=1, keepdims=True)
        p = jnp.exp(s - m)
        den = jnp.sum(p, axis=1, keepdims=True)
        outs.append((_dot(p.astype(BF16), v) / den).astype(BF16))
    o = jnp.concatenate(outs, axis=1)
    y = x + _dot(o, wo_s[...])
    o_ref[...] = y
    if emit_next:
        rest[0][...] = _rms(y, gn_ref[...]).astype(BF16)


def cross_attention(x, g, wq3, wo3, layer, kv, seq, mem_len, next_gain=None, *, tm=512, name="xattn"):
    t, d = x.shape
    nsb = seq // tm
    assert seq % tm == 0
    emit_next = next_gain is not None
    vm = 4 * _nbytes((tm, d), F32) + 3 * _nbytes((d, X_W), F32) * 2 + 8 * _nbytes((tm, d), F32) + (8 << 20)
    row = pl.BlockSpec((tm, d), lambda i: (i, 0))
    vec = pl.BlockSpec((1, d), lambda i: (0, 0))
    out = pl.pallas_call(
        functools.partial(_xattn_kernel, emit_next=emit_next),
        grid=(t // tm,),
        in_specs=[row, vec,
                  pl.BlockSpec((None, d, X_W), lambda i: (layer, 0, 0)),
                  pl.BlockSpec((mem_len, 2 * X_W), lambda i: (i // nsb, 0)),
                  pl.BlockSpec((None, X_W, d), lambda i: (layer, 0, 0)),
                  vec],
        out_specs=[row, row] if emit_next else [row],
        out_shape=[jax.ShapeDtypeStruct((t, d), F32)] + ([jax.ShapeDtypeStruct((t, d), BF16)] if emit_next else []),
        scratch_shapes=[pltpu.VMEM((d, X_W), BF16), pltpu.VMEM((X_W, d), BF16)],
        compiler_params=_params(("arbitrary",), vm),
        name=name,
    )(x, g.reshape(1, d), wq3, kv, wo3, (next_gain if emit_next else g).reshape(1, d))
    return tuple(out) if emit_next else out[0]


def _conv_kernel(xp_ref, xc_ref, xn_ref, w_ref, o_ref, *, ts, ns, halo, q_scale, n_q_blocks):
    i = pl.program_id(1)
    j = pl.program_id(2)
    pad = B_CONV // 2
    xp = jnp.where(i > 0, xp_ref[halo - pad:, :].astype(F32), 0.0)
    xn = jnp.where(i < ns - 1, xn_ref[:pad, :].astype(F32), 0.0)
    xx = jnp.concatenate([xp, xc_ref[...].astype(F32), xn], axis=0)
    w = w_ref[...]
    acc = xx[0:ts, :] * w[0:1, :]
    for tap in range(1, B_CONV):
        acc = acc + xx[tap:tap + ts, :] * w[tap:tap + 1, :]
    y = acc * jax.nn.sigmoid(acc)
    y = y * jnp.where(j < n_q_blocks, q_scale, 1.0)
    o_ref[...] = y.astype(o_ref.dtype)


def conv_silu(z, conv_w8, bsz, seq, *, ts=512, tc=512, name="conv_silu"):
    t = z.shape[0]
    halo = 16
    ns = seq // ts
    nh = seq // halo
    cb = COL_QKB // tc
    assert COL_QKB % tc == 0 and seq % ts == 0 and B_W % tc == 0
    return pl.pallas_call(
        functools.partial(_conv_kernel, ts=ts, ns=ns, halo=halo, q_scale=B_HEAD_DIM ** -0.5, n_q_blocks=B_W // tc),
        grid=(bsz, ns, 2 * B_W // tc),
        in_specs=[pl.BlockSpec((halo, tc), lambda b, i, j: (b * nh + jnp.maximum(i * (ts // halo) - 1, 0), cb + j)),
                  pl.BlockSpec((ts, tc), lambda b, i, j: (b * ns + i, cb + j)),
                  pl.BlockSpec((halo, tc), lambda b, i, j: (b * nh + jnp.minimum((i + 1) * (ts // halo), nh - 1), cb + j)),
                  pl.BlockSpec((8, tc), lambda b, i, j: (0, j))],
        out_specs=pl.BlockSpec((ts, tc), lambda b, i, j: (b * ns + i, j)),
        out_shape=jax.ShapeDtypeStruct((t, 2 * B_W), BF16),
        compiler_params=_params(("arbitrary",) * 3, 32 << 20),
        name=name,
    )(z, z, z, conv_w8)


def _log_sigmoid(x):
    return jnp.minimum(x, 0.0) - jnp.log(1.0 + jnp.exp(-jnp.abs(x)))


def _mlstm_chunk(q, k, v, i_col, f_col, i_row, f_row, c_ref, n_ref, m_ref, rev):
    L = q.shape[0]
    logf_c = _log_sigmoid(f_col)
    logf_r = _log_sigmoid(f_row)
    row = lax.broadcasted_iota(jnp.int32, (L, L), 0)
    col = lax.broadcasted_iota(jnp.int32, (L, L), 1)
    causal = (col >= row) if rev else (col <= row)
    causal_t = (row >= col) if rev else (row <= col)
    b_col = jnp.sum(jnp.where(causal, logf_r, 0.0), axis=1, keepdims=True)
    b_row = jnp.sum(jnp.where(causal_t, logf_c, 0.0), axis=0, keepdims=True)
    b_all = jnp.sum(logf_r, axis=1, keepdims=True)
    m_st = m_ref[...]
    c_st = c_ref[...]
    n_st = n_ref[...]
    logd = jnp.where(causal, b_col - b_row + i_row, NEG)
    m_inter = b_col + m_st
    mt = jnp.maximum(m_inter, jnp.max(logd, axis=1, keepdims=True))
    s = _dot_nt(q, k) * jnp.exp(logd - mt)
    sc = jnp.exp(m_inter - mt)
    num = _dot(s.astype(BF16), v) + sc * _dot(q, c_st.astype(BF16))
    den = jnp.sum(s, axis=1, keepdims=True) + sc * jnp.sum(q.astype(F32) * n_st, axis=1, keepdims=True)
    h = num / jnp.maximum(jnp.abs(den), jnp.exp(-mt))
    logw = b_all - b_col + i_col
    m_new = jnp.maximum(b_all + m_st, jnp.max(logw, axis=0, keepdims=True))
    wgt = jnp.exp(logw - m_new)
    dec = jnp.exp(b_all + m_st - m_new)
    kw = k.astype(F32) * wgt
    c_ref[...] = dec * c_st + _dot_tn(kw.astype(BF16), v)
    n_ref[...] = dec * n_st + jnp.sum(kw, axis=0, keepdims=True)
    m_ref[...] = m_new
    return h


def _mlstm_kernel(bias_ref, qf_ref, kf_ref, vf0_ref, vf1_ref, gcf_ref, grf_ref,
                  qb_ref, kb_ref, vb0_ref, vb1_ref, gcb_ref, grb_ref,
                  hf_ref, hb_ref, c_ref, n_ref, m_ref):
    @pl.when(pl.program_id(1) == 0)
    def _():
        c_ref[...] = jnp.zeros_like(c_ref)
        n_ref[...] = jnp.zeros_like(n_ref)
        m_ref[...] = jnp.zeros_like(m_ref)

    lane = lax.broadcasted_iota(jnp.int32, (1, LANES), 1)
    dh = B_HEAD_DIM
    for d, (q_ref, k_ref, v0_ref, v1_ref, gc_ref, gr_ref, h_ref) in enumerate((
            (qf_ref, kf_ref, vf0_ref, vf1_ref, gcf_ref, grf_ref, hf_ref),
            (qb_ref, kb_ref, vb0_ref, vb1_ref, gcb_ref, grb_ref, hb_ref))):
        gc = gc_ref[...]
        for hd in range(B_HEADS):
            ci = 2 * d * B_HEADS + hd
            cf = ci + B_HEADS
            i_col = jnp.sum(jnp.where(lane == ci, gc, 0.0), axis=1, keepdims=True) + bias_ref[ci]
            f_col = jnp.sum(jnp.where(lane == cf, gc, 0.0), axis=1, keepdims=True) + bias_ref[cf]
            i_row = gr_ref[ci:ci + 1, :] + bias_ref[ci]
            f_row = gr_ref[cf:cf + 1, :] + bias_ref[cf]
            v_ref = v0_ref if hd < B_HEADS // 2 else v1_ref
            vo = (hd % (B_HEADS // 2)) * dh
            st = d * B_HEADS + hd
            h = _mlstm_chunk(q_ref[:, hd * dh:(hd + 1) * dh], k_ref[:, hd * dh:(hd + 1) * dh],
                             v_ref[:, vo:vo + dh], i_col, f_col, i_row, f_row,
                             c_ref.at[st], n_ref.at[st], m_ref.at[st], rev=bool(d))
            h_ref[:, hd * dh:(hd + 1) * dh] = h


def mlstm(qk, z, g_cols, g_rows, b_gate, bsz, seq, *, chunk=256, name="mlstm"):
    t = qk.shape[0]
    nc = seq // chunk
    assert seq % chunk == 0
    half = B_W // 2
    vcol = COL_VB // half
    assert COL_VB % half == 0
    fw = lambda b, c, s: b * nc + c
    bw = lambda b, c, s: b * nc + (nc - 1 - c)

    def specs(rowf):
        return [pl.BlockSpec((chunk, B_W), lambda b, c, s: (rowf(b, c, s), 0)),
                pl.BlockSpec((chunk, B_W), lambda b, c, s: (rowf(b, c, s), 1)),
                pl.BlockSpec((chunk, half), lambda b, c, s: (rowf(b, c, s), vcol)),
                pl.BlockSpec((chunk, half), lambda b, c, s: (rowf(b, c, s), vcol + 1)),
                pl.BlockSpec((chunk, LANES), lambda b, c, s: (rowf(b, c, s), 0)),
                pl.BlockSpec((16, chunk), lambda b, c, s: (0, rowf(b, c, s)))]

    grid_spec = pltpu.PrefetchScalarGridSpec(
        num_scalar_prefetch=1,
        grid=(bsz, nc),
        in_specs=specs(fw) + specs(bw),
        out_specs=[pl.BlockSpec((chunk, B_W), lambda b, c, s: (fw(b, c, s), 0)),
                   pl.BlockSpec((chunk, B_W), lambda b, c, s: (bw(b, c, s), 0))],
        scratch_shapes=[pltpu.VMEM((2 * B_HEADS, B_HEAD_DIM, B_HEAD_DIM), F32),
                        pltpu.VMEM((2 * B_HEADS, 1, B_HEAD_DIM), F32),
                        pltpu.VMEM((2 * B_HEADS, 1, 1), F32)],
    )
    args = (qk, qk, z, z, g_cols, g_rows)
    return pl.pallas_call(
        _mlstm_kernel,
        grid_spec=grid_spec,
        out_shape=[jax.ShapeDtypeStruct((t, B_W), F32), jax.ShapeDtypeStruct((t, B_W), F32)],
        compiler_params=_params(("arbitrary", "arbitrary"), 48 << 20),
        name=name,
    )(b_gate.astype(F32), *args, *args)


def _mlstm_out_kernel(hf_ref, hb_ref, o0_ref, o1_ref, gain_ref, y_ref):
    dh = B_HEAD_DIM
    for hd in range(B_HEADS):
        sl = slice(hd * dh, (hd + 1) * dh)
        h = hf_ref[:, sl] + hb_ref[:, sl]
        hc = h - jnp.mean(h, axis=1, keepdims=True)
        y = hc * lax.rsqrt(jnp.mean(hc * hc, axis=1, keepdims=True) + EPS) * gain_ref[:, sl]
        o_ref = o0_ref if hd < B_HEADS // 2 else o1_ref
        oo = (hd % (B_HEADS // 2)) * dh
        y_ref[:, sl] = (jax.nn.sigmoid(o_ref[:, oo:oo + dh].astype(F32)) * y).astype(y_ref.dtype)


def mlstm_output(hf, hb, z, gain, *, tm=512, name="mlstm_out"):
    t = hf.shape[0]
    half = B_W // 2
    ocol = COL_OB // half
    assert COL_OB % half == 0 and t % tm == 0
    return pl.pallas_call(
        _mlstm_out_kernel,
        grid=(t // tm,),
        in_specs=[pl.BlockSpec((tm, B_W), lambda i: (i, 0)),
                  pl.BlockSpec((tm, B_W), lambda i: (i, 0)),
                  pl.BlockSpec((tm, half), lambda i: (i, ocol)),
                  pl.BlockSpec((tm, half), lambda i: (i, ocol + 1)),
                  pl.BlockSpec((1, B_W), lambda i: (0, 0))],
        out_specs=pl.BlockSpec((tm, B_W), lambda i: (i, 0)),
        out_shape=jax.ShapeDtypeStruct((t, B_W), BF16),
        compiler_params=_params(("arbitrary",), 32 << 20),
        name=name,
    )(hf, hb, z, z, gain.reshape(1, B_W))


def _tile_state(na_ref, th_ref):
    i = pl.program_id(1)
    active = i < na_ref[0]
    half_only = th_ref[jnp.minimum(i, na_ref[0] - 1)] == 1
    return active, half_only


def _up_kernel(te_ref, na_ref, th_ref, x_ref, wg_ref, wu_ref, h_ref, *, tm):
    active, half_only = _tile_state(na_ref, th_ref)

    def compute(rows):
        x = x_ref[0:rows, :]
        g = _dot(x, wg_ref[...].astype(BF16))
        u = _dot(x, wu_ref[...].astype(BF16))
        h_ref[0:rows, :] = (g * jax.nn.sigmoid(g) * u).astype(h_ref.dtype)
        if rows < tm:
            h_ref[rows:, :] = jnp.zeros((tm - rows, h_ref.shape[1]), h_ref.dtype)

    pl.when(active & jnp.logical_not(half_only))(lambda: compute(tm))
    pl.when(active & half_only)(lambda: compute(tm // 2))

    @pl.when(jnp.logical_not(active))
    def _():
        h_ref[...] = jnp.zeros_like(h_ref)


def swiglu_up(x, w_gu3, tile_expert, n_active, tile_half, *, tm=512, tf=512, name="ffn_up"):
    p, d = x.shape
    f = w_gu3.shape[2] // 2
    nt = p // tm
    nj = f // tf
    assert p % tm == 0 and f % tf == 0
    row = lambda i, na: jnp.minimum(i, na[0] - 1)
    grid_spec = pltpu.PrefetchScalarGridSpec(
        num_scalar_prefetch=3,
        grid=(nj, nt),
        in_specs=[pl.BlockSpec((tm, d), lambda j, i, te, na, th: (row(i, na), 0)),
                  pl.BlockSpec((None, d, tf), lambda j, i, te, na, th: (te[row(i, na)], 0, j)),
                  pl.BlockSpec((None, d, tf), lambda j, i, te, na, th: (te[row(i, na)], 0, nj + j))],
        out_specs=pl.BlockSpec((tm, tf), lambda j, i, te, na, th: (i, j)),
    )
    vm = 2 * _nbytes((tm, d), BF16) + 2 * 3 * _nbytes((d, tf), F32) + 8 * _nbytes((tm, tf), F32) + (4 << 20)
    return pl.pallas_call(
        functools.partial(_up_kernel, tm=tm),
        grid_spec=grid_spec,
        out_shape=jax.ShapeDtypeStruct((p, f), BF16),
        compiler_params=_params(("arbitrary", "arbitrary"), vm),
        name=name,
    )(tile_expert, n_active, tile_half, x, w_gu3, w_gu3)


def _down_kernel(te_ref, na_ref, th_ref, h_ref, w_ref, *rest, tm, has_resid):
    o_ref = rest[-1]
    active, half_only = _tile_state(na_ref, th_ref)

    def compute(rows):
        o = _dot(h_ref[0:rows, :], w_ref[...].astype(BF16))
        if has_resid:
            o = o + rest[0][0:rows, :]
        o_ref[0:rows, :] = o
        if rows < tm:
            o_ref[rows:, :] = jnp.zeros((tm - rows, o_ref.shape[1]), o_ref.dtype)

    pl.when(active & jnp.logical_not(half_only))(lambda: compute(tm))
    pl.when(active & half_only)(lambda: compute(tm // 2))

    @pl.when(jnp.logical_not(active))
    def _():
        o_ref[...] = jnp.zeros_like(o_ref)


def swiglu_down(h, w_d3, tile_expert, n_active, tile_half, resid=None, *, tm=512, tn=512, name="ffn_down"):
    p, f = h.shape
    d = w_d3.shape[2]
    nt = p // tm
    assert p % tm == 0 and d % tn == 0
    row = lambda i, na: jnp.minimum(i, na[0] - 1)
    in_specs = [pl.BlockSpec((tm, f), lambda n, i, te, na, th: (row(i, na), 0)),
                pl.BlockSpec((None, f, tn), lambda n, i, te, na, th: (te[row(i, na)], 0, n))]
    args = [h, w_d3]
    if resid is not None:
        in_specs.append(pl.BlockSpec((tm, tn), lambda n, i, te, na, th: (row(i, na), n)))
        args.append(resid)
    grid_spec = pltpu.PrefetchScalarGridSpec(
        num_scalar_prefetch=3,
        grid=(d // tn, nt),
        in_specs=in_specs,
        out_specs=pl.BlockSpec((tm, tn), lambda n, i, te, na, th: (i, n)),
    )
    vm = 2 * _nbytes((tm, f), BF16) + 2 * _nbytes((f, tn), F32) + _nbytes((f, tn), BF16) + 8 * _nbytes((tm, tn), F32) + (4 << 20)
    return pl.pallas_call(
        functools.partial(_down_kernel, tm=tm, has_resid=resid is not None),
        grid_spec=grid_spec,
        out_shape=jax.ShapeDtypeStruct((p, d), F32),
        compiler_params=_params(("arbitrary", "arbitrary"), vm),
        name=name,
    )(tile_expert, n_active, tile_half, *args)


def _router_kernel(x_ref, g_ref, w_ref, mi_ref, mf_ref, cnt_ref, carry_ref, *, tm):
    @pl.when(pl.program_id(0) == 0)
    def _():
        carry_ref[...] = jnp.zeros_like(carry_ref)

    u = _rms(x_ref[...], g_ref[...])
    w = w_ref[...]
    u_hi = u.astype(BF16)
    w_hi = w.astype(BF16)
    u_lo = (u - u_hi.astype(F32)).astype(BF16)
    w_lo = (w - w_hi.astype(F32)).astype(BF16)
    logits = _dot(u_hi, w_hi) + (_dot(u_hi, w_lo) + _dot(u_lo, w_hi))
    lane = lax.broadcasted_iota(jnp.int32, (tm, LANES), 1)
    logits = jnp.where(lane < N_EXPERTS, logits, -jnp.inf)
    v1 = jnp.max(logits, axis=1, keepdims=True)
    i1 = jnp.min(jnp.where(logits == v1, lane, LANES), axis=1, keepdims=True)
    oh1 = lane == i1
    rest = jnp.where(oh1, -jnp.inf, logits)
    v2 = jnp.max(rest, axis=1, keepdims=True)
    i2 = jnp.min(jnp.where(rest == v2, lane, LANES), axis=1, keepdims=True)
    oh2 = lane == i2
    e = jnp.exp(v2 - v1)
    g1 = 1.0 / (1.0 + e)
    g2 = e * g1
    cnt = (oh1 | oh2).astype(F32)
    r = lax.broadcasted_iota(jnp.int32, (tm, tm), 0)
    c = lax.broadcasted_iota(jnp.int32, (tm, tm), 1)
    before = (c < r).astype(BF16)
    excl = _dot(before, cnt.astype(BF16)) + carry_ref[0:1, :]
    r1 = jnp.sum(jnp.where(oh1, excl, 0.0), axis=1, keepdims=True).astype(jnp.int32)
    r2 = jnp.sum(jnp.where(oh2, excl, 0.0), axis=1, keepdims=True).astype(jnp.int32)
    mi_ref[...] = jnp.where(lane == 0, i1, jnp.where(lane == 1, i2, jnp.where(lane == 2, r1, jnp.where(lane == 3, r2, 0))))
    mf_ref[...] = jnp.where(lane == 0, g1, jnp.where(lane == 1, g2, 0.0))
    carry_ref[...] = carry_ref[...] + jnp.sum(cnt, axis=0, keepdims=True)
    cnt_ref[...] = carry_ref[...]


def moe_router(x, g, w_router3, layer, *, tm=512, name="moe_router"):
    t, d = x.shape
    assert t % tm == 0
    vm = 6 * _nbytes((tm, d), F32) + 4 * _nbytes((d, LANES), F32) + 8 * _nbytes((tm, tm), F32) + (4 << 20)
    return pl.pallas_call(
        functools.partial(_router_kernel, tm=tm),
        grid=(t // tm,),
        in_specs=[pl.BlockSpec((tm, d), lambda i: (i, 0)),
                  pl.BlockSpec((1, d), lambda i: (0, 0)),
                  pl.BlockSpec((None, d, LANES), lambda i: (layer, 0, 0))],
        out_specs=[pl.BlockSpec((tm, LANES), lambda i: (i, 0)),
                   pl.BlockSpec((tm, LANES), lambda i: (i, 0)),
                   pl.BlockSpec((8, LANES), lambda i: (0, 0))],
        out_shape=[jax.ShapeDtypeStruct((t, LANES), jnp.int32),
                   jax.ShapeDtypeStruct((t, LANES), F32),
                   jax.ShapeDtypeStruct((8, LANES), F32)],
        scratch_shapes=[pltpu.VMEM((8, LANES), F32)],
        compiler_params=_params(("arbitrary",), vm),
        name=name,
    )(x, g.reshape(1, d), w_router3)


def _row_copy(src_hbm, dst_vmem, sem, src_row, dst_row):
    return pltpu.make_async_copy(src_hbm.at[pl.ds(src_row, 1), :], dst_vmem.at[pl.ds(dst_row, 1), :], sem)


GATHER_UNROLL = 8


def _start_rows(src_hbm, idx_refs, dst_refs, sem, n):
    for r in range(n):
        for idx_ref, dst in zip(idx_refs, dst_refs):
            _row_copy(src_hbm, dst, sem, idx_ref[0, 0, r], r).start(priority=r % 2)


def _wait_rows(src_hbm, dst_refs, sem, n):
    def body(r, carry):
        for dst in dst_refs:
            _row_copy(src_hbm, dst, sem, 0, r).wait()
        return carry

    lax.fori_loop(0, n, body, 0, unroll=GATHER_UNROLL)


def _dispatch_kernel(tok_ref, tok_next_ref, x_hbm, g_ref, o_ref, rows_ref, sems, *, tg):
    i = pl.program_id(0)
    slot = i % 2

    @pl.when(i == 0)
    def _():
        _start_rows(x_hbm, [tok_ref], [rows_ref.at[0]], sems.at[0], tg)

    @pl.when(i + 1 < pl.num_programs(0))
    def _():
        _start_rows(x_hbm, [tok_next_ref], [rows_ref.at[1 - slot]], sems.at[1 - slot], tg)

    _wait_rows(x_hbm, [rows_ref.at[slot]], sems.at[slot], tg)
    o_ref[...] = _rms(rows_ref[slot], g_ref[...]).astype(o_ref.dtype)


def moe_dispatch(x, g, token_of, *, tg=256, name="moe_dispatch"):
    t, d = x.shape
    p = token_of.shape[0]
    assert p % tg == 0
    nt = p // tg
    vm = 4 * _nbytes((tg, d), F32) + 2 * _nbytes((tg, d), BF16) + (4 << 20)
    tok3 = token_of.reshape(nt, 1, tg)
    return pl.pallas_call(
        functools.partial(_dispatch_kernel, tg=tg),
        grid=(nt,),
        in_specs=[pl.BlockSpec((1, 1, tg), lambda i: (i, 0, 0), memory_space=pltpu.SMEM),
                  pl.BlockSpec((1, 1, tg), lambda i: (jnp.minimum(i + 1, nt - 1), 0, 0), memory_space=pltpu.SMEM),
                  pl.BlockSpec(memory_space=pl.ANY),
                  pl.BlockSpec((1, d), lambda i: (0, 0))],
        out_specs=pl.BlockSpec((tg, d), lambda i: (i, 0)),
        out_shape=jax.ShapeDtypeStruct((p, d), BF16),
        scratch_shapes=[pltpu.VMEM((2, tg, d), F32), pltpu.SemaphoreType.DMA((2,))],
        compiler_params=_params(("arbitrary",), vm),
        name=name,
    )(tok3, tok3, x, g.reshape(1, d))


def _combine_kernel(p1_ref, p2_ref, p1n_ref, p2n_ref, o_hbm, mf_ref, x_ref, gf_ref, y_ref, rows_ref, sems, *, tc,
                    final_norm):
    i = pl.program_id(0)
    slot = i % 2
    dst = lambda s: [rows_ref.at[s, 0], rows_ref.at[s, 1]]

    @pl.when(i == 0)
    def _():
        _start_rows(o_hbm, [p1_ref, p2_ref], dst(0), sems.at[0], tc)

    @pl.when(i + 1 < pl.num_programs(0))
    def _():
        _start_rows(o_hbm, [p1n_ref, p2n_ref], dst(1 - slot), sems.at[1 - slot], tc)

    _wait_rows(o_hbm, dst(slot), sems.at[slot], tc)
    mf = mf_ref[...]
    y = x_ref[...] + mf[:, 0:1] * rows_ref[slot, 0] + mf[:, 1:2] * rows_ref[slot, 1]
    y_ref[...] = _rms(y, gf_ref[...]) if final_norm else y


def moe_combine(o, pos1, pos2, gates, x, final_gain=None, *, tc=256, name="moe_combine"):
    t, d = x.shape
    final_norm = final_gain is not None
    assert t % tc == 0
    nt = t // tc
    vm = 4 * _nbytes((tc, d), F32) + 6 * _nbytes((tc, d), F32) + (4 << 20)
    cur = pl.BlockSpec((1, 1, tc), lambda i: (i, 0, 0), memory_space=pltpu.SMEM)
    nxt = pl.BlockSpec((1, 1, tc), lambda i: (jnp.minimum(i + 1, nt - 1), 0, 0), memory_space=pltpu.SMEM)
    p1 = pos1.reshape(nt, 1, tc)
    p2 = pos2.reshape(nt, 1, tc)
    return pl.pallas_call(
        functools.partial(_combine_kernel, tc=tc, final_norm=final_norm),
        grid=(nt,),
        in_specs=[cur, cur, nxt, nxt,
                  pl.BlockSpec(memory_space=pl.ANY),
                  pl.BlockSpec((tc, LANES), lambda i: (i, 0)),
                  pl.BlockSpec((tc, d), lambda i: (i, 0)),
                  pl.BlockSpec((1, d), lambda i: (0, 0))],
        out_specs=pl.BlockSpec((tc, d), lambda i: (i, 0)),
        out_shape=jax.ShapeDtypeStruct((t, d), F32),
        scratch_shapes=[pltpu.VMEM((2, 2, tc, d), F32), pltpu.SemaphoreType.DMA((2,))],
        compiler_params=_params(("arbitrary",), vm),
        name=name,
    )(p1, p2, p1, p2, o, gates, x, (final_gain if final_norm else jnp.ones((d,), F32)).reshape(1, d))


def _rope_tables(seq):
    pos = jnp.arange(seq, dtype=F32)
    inv = jnp.power(ROPE_THETA, -jnp.arange(0, ROT_DIM, 2, dtype=F32) / ROT_DIM)
    ang = pos[:, None] * inv[None, :]
    cos, sin = jnp.cos(ang), jnp.sin(ang)
    half = ROT_DIM // 2
    zeros = jnp.zeros((seq, LANES - ROT_DIM), F32)
    c = jnp.concatenate([cos, cos, jnp.ones_like(zeros)], axis=1)
    s1 = jnp.concatenate([-sin, jnp.zeros((seq, half), F32), zeros], axis=1)
    s2 = jnp.concatenate([jnp.zeros((seq, half), F32), sin, zeros], axis=1)
    return c, s1, s2


def _moe_plan(meta_i, counts, t, tm):
    eid = meta_i[:, 0:2]
    rank = meta_i[:, 2:4]
    cnt = counts[0, :N_EXPERTS].astype(jnp.int32)
    padded = ((cnt + tm - 1) // tm) * tm
    ends = jnp.cumsum(padded)
    starts = ends - padded
    pos = starts[eid] + rank
    p_max = 2 * t + N_EXPERTS * tm
    nt = p_max // tm
    n_active = jnp.maximum(ends[-1] // tm, 1)
    tile_start = jnp.minimum(jnp.arange(nt, dtype=jnp.int32), n_active - 1) * tm
    owner = jnp.sum((tile_start[:, None] >= ends[None, :]).astype(jnp.int32), axis=1)
    tile_expert = jnp.minimum(owner, N_EXPERTS - 1)
    tok = jnp.broadcast_to(jnp.arange(t, dtype=jnp.int32)[:, None], (t, 2))
    token_of = jnp.zeros((p_max,), jnp.int32).at[pos.reshape(-1)].set(tok.reshape(-1))
    valid = jnp.clip((starts + cnt)[tile_expert] - tile_start, 0, tm)
    tile_half = (valid <= tm // 2).astype(jnp.int32)
    return pos[:, 0], pos[:, 1], token_of, tile_expert, n_active.reshape(1).astype(jnp.int32), tile_half


FFN_TILE = 512
MOE_UP_COLS = 1024
DENSE_UP_TILE = 1024


def even_mixer(xs, g, w_in3, gate_w3, conv_w8, b_gate, sink, head_gain, w_out3, j, rope, bsz, seq):
    z = norm_matmul(xs, g, w_in3, j, EVEN_MAIN, rope=rope, seq=seq, n_rope=(A_Q_W + A_KV_W) // HEAD_DIM,
                    n_scale=A_Q_W // HEAD_DIM, scale=HEAD_DIM ** -0.5, name="even_in")
    g_cols = norm_matmul(xs, g, gate_w3, j, LANES, out_dtype=F32, name="even_gates")
    g_rows = g_cols[:, :4 * B_HEADS].T
    ya = window_attention(z, sink, bsz, seq)
    qk = conv_silu(z, conv_w8, bsz, seq)
    hf, hb = mlstm(qk, z, g_cols, g_rows, b_gate, bsz, seq)
    yb = mlstm_output(hf, hb, z, head_gain)
    return matmul_resid([ya, yb], w_out3, j, xs, name="even_out")


def odd_mixer(xs, g, w_in3, w_out3, j, rope, bsz, seq):
    d = xs.shape[1]
    z = norm_matmul(xs, g, w_in3, j, 3 * d, rope=rope, seq=seq, n_rope=2 * C_HEADS, n_scale=C_HEADS,
                    scale=HEAD_DIM ** -0.5 * math.log2(math.e), name="odd_in")
    y = dilated_attention(z, bsz, seq)
    return matmul_resid([y], w_out3, j, xs, name="odd_out")


def dense_ffn(xs, u, w_gu3, w_d3, j):
    t = xs.shape[0]
    plan = lambda tm: (jnp.full((t // tm,), j, jnp.int32), jnp.full((1,), t // tm, jnp.int32),
                       jnp.zeros((t // tm,), jnp.int32))
    h = swiglu_up(u, w_gu3, *plan(DENSE_UP_TILE), tm=DENSE_UP_TILE, name="ffn_up")
    return swiglu_down(h, w_d3, *plan(FFN_TILE), resid=xs, tm=FFN_TILE, name="ffn_down")


def moe_ffn(xs, g, router_w3, w_gu3, w_d3, j, final_gain=None):
    t = xs.shape[0]
    meta_i, gates, counts = moe_router(xs, g, router_w3, j)
    pos1, pos2, token_of, tile_expert, n_active, tile_half = _moe_plan(meta_i, counts, t, FFN_TILE)
    xd = moe_dispatch(xs, g, token_of)
    experts = tile_expert + j * N_EXPERTS
    h = swiglu_up(xd, w_gu3, experts, n_active, tile_half, tm=FFN_TILE, tf=MOE_UP_COLS, name="moe_up")
    o = swiglu_down(h, w_d3, experts, n_active, tile_half, tm=FFN_TILE, name="moe_down")
    return moe_combine(o, pos1, pos2, gates, xs, final_gain)


def kernel(x, mem, ln_mix, ln_xattn, ln_mem, ln_ffn, ln_final, ev_w_in, ev_b_gate, ev_conv, ev_sink, ev_head_norm,
           ev_w_out, ffn_w_gu, ffn_w_down, od_w_in, od_w_out, moe_router, moe_w_gu, moe_w_down, x_wq, x_wkv, x_wo):
    bsz, seq, d = x.shape
    mem_len = mem.shape[1]
    depth = ln_mix.shape[0]
    t = bsz * seq
    rope = _rope_tables(seq)
    xs = x.reshape(t, d)
    memf = mem.reshape(bsz * mem_len, d)
    n_odd = od_w_in.shape[0]
    moe_gu = moe_w_gu.reshape(n_odd * N_EXPERTS, d, 2 * FFN_DIM)
    moe_dn = moe_w_down.reshape(n_odd * N_EXPERTS, FFN_DIM, d)
    router_w = jnp.pad(moe_router, ((0, 0), (0, 0), (0, LANES - N_EXPERTS)))
    gate_w = jnp.pad(ev_w_in[:, :, EVEN_MAIN:], ((0, 0), (0, 0), (0, LANES - 4 * B_HEADS)))
    conv_w8 = jnp.pad(ev_conv, ((0, 0), (0, 8 - B_CONV), (0, 0)))

    for layer in range(depth):
        j = layer // 2
        if layer % 2 == 0:
            xs = even_mixer(xs, ln_mix[layer], ev_w_in, gate_w, conv_w8[j], ev_b_gate[j], ev_sink[j],
                            ev_head_norm[j], ev_w_out, j, rope, bsz, seq)
        else:
            xs = odd_mixer(xs, ln_mix[layer], od_w_in, od_w_out, j, rope, bsz, seq)
        kv = norm_matmul(memf, ln_mem[layer], x_wkv, layer, 2 * X_W, name="mem_kv")
        last = layer == depth - 1
        if layer % 2 == 0:
            xs, u = cross_attention(xs, ln_xattn[layer], x_wq, x_wo, layer, kv, seq, mem_len, ln_ffn[layer])
            xs = dense_ffn(xs, u, ffn_w_gu, ffn_w_down, j)
            if last:
                xs = rmsnorm(xs, ln_final, F32, name="final_norm")
        else:
            xs = cross_attention(xs, ln_xattn[layer], x_wq, x_wo, layer, kv, seq, mem_len)
            xs = moe_ffn(xs, ln_ffn[layer], router_w, moe_gu, moe_dn, j, ln_final if last else None)
    return xs.reshape(bsz, seq, d)
```

```python
import functools
import math

import jax
import jax.numpy as jnp
from jax import lax
from jax.experimental import pallas as pl
from jax.experimental.pallas import tpu as pltpu

D_MODEL = 2048
HEAD_DIM = 128
A_Q_HEADS = 8
A_KV_HEADS = 2
A_HALF_WINDOW = 128
B_HEADS = 4
B_HEAD_DIM = 256
B_CONV = 5
C_HEADS = D_MODEL // HEAD_DIM
C_PATTERNS = ((128, 1), (512, 4), (2048, 16))
X_HEADS = 4
FFN_DIM = 7168
N_EXPERTS = 8
ROPE_THETA = 500000.0
ROT_DIM = HEAD_DIM // 4
EPS = 1e-6

A_Q_W = A_Q_HEADS * HEAD_DIM
A_KV_W = A_KV_HEADS * HEAD_DIM
B_W = B_HEADS * B_HEAD_DIM
X_W = X_HEADS * HEAD_DIM
EVEN_MAIN = A_Q_W + 2 * A_KV_W + 4 * B_W
COL_KA = A_Q_W
COL_VA = A_Q_W + A_KV_W
COL_QKB = A_Q_W + 2 * A_KV_W
COL_VB = COL_QKB + 2 * B_W
COL_OB = COL_VB + B_W

LANES = 128
V7X_VMEM_BYTES = 64 * 1024 * 1024
VMEM_CAP = V7X_VMEM_BYTES - 8 * 1024 * 1024

NEG = -1e30
BF16 = jnp.bfloat16
F32 = jnp.float32


def _params(sem, vmem_bytes):
    return pltpu.CompilerParams(dimension_semantics=sem, vmem_limit_bytes=int(min(VMEM_CAP, vmem_bytes)))


def _nbytes(shape, dtype):
    return math.prod(shape) * jnp.dtype(dtype).itemsize


def _dot(a, b):
    return jnp.dot(a, b, preferred_element_type=F32)


def _dot_nt(a, b):
    return lax.dot_general(a, b, (((1,), (1,)), ((), ())), preferred_element_type=F32)


def _dot_tn(a, b):
    return lax.dot_general(a, b, (((0,), (0,)), ((), ())), preferred_element_type=F32)


def _rms(x, g):
    return x * lax.rsqrt(jnp.mean(x * x, axis=-1, keepdims=True) + EPS) * g


def _rope_tile(z, c, s1, s2):
    return z * c + pltpu.roll(z, LANES - ROT_DIM // 2, 1) * s1 + pltpu.roll(z, ROT_DIM // 2, 1) * s2


def _norm_mm_kernel(x_ref, g_ref, w_ref, c_ref, s1_ref, s2_ref, o_ref, xn_ref, *, tn, n_steps, rope_from, n_rope,
                    n_scale, scale):
    j = pl.program_id(1)

    @pl.when(j == 0)
    def _():
        xn_ref[...] = _rms(x_ref[...], g_ref[...]).astype(BF16)

    heads = tn // LANES
    half = min(tn, 2 * LANES)

    def step(kinds):
        xn = xn_ref[...]
        for c0 in range(0, tn, half):
            z = _dot(xn, w_ref[:, c0:c0 + half].astype(BF16))
            for hh in range(c0 // LANES, (c0 + half) // LANES):
                zt = z[:, hh * LANES - c0:(hh + 1) * LANES - c0]
                rotate, scaled = kinds[hh]
                if rotate:
                    zt = _rope_tile(zt, c_ref[...], s1_ref[...], s2_ref[...])
                if scaled:
                    zt = zt * scale
                o_ref[:, hh * LANES:(hh + 1) * LANES] = zt.astype(o_ref.dtype)

    kinds_of = lambda jj: tuple((rope_from <= jj * heads + hh < n_rope, jj * heads + hh < n_scale) for hh in range(heads))
    runs = []
    for jj in range(n_steps):
        if runs and runs[-1][0] == kinds_of(jj):
            runs[-1][2] = jj
        else:
            runs.append([kinds_of(jj), jj, jj])
    for kinds, lo, hi in runs:
        if len(runs) == 1:
            step(kinds)
        else:
            pl.when((j >= lo) & (j <= hi))(functools.partial(step, kinds))


def norm_matmul(x, g, w3, widx, n_out, *, rope=None, rope_from=0, n_rope=0, n_scale=0, scale=1.0, seq=None, tm=1024,
                tn=512, out_dtype=BF16, name="norm_mm"):
    m, k = x.shape
    tm = min(tm, m)
    tn = min(tn, n_out)
    assert m % tm == 0 and n_out % tn == 0 and tn % LANES == 0
    if rope is None:
        dummy = jnp.zeros((8, LANES), F32)
        rope = (dummy, dummy, dummy)
        rspec = pl.BlockSpec((8, LANES), lambda i, j: (0, 0))
    else:
        nsb = seq // tm
        assert seq % tm == 0
        rspec = pl.BlockSpec((tm, LANES), lambda i, j: (i % nsb, 0))
    vm = 2 * _nbytes((tm, k), F32) + _nbytes((tm, k), BF16) + 2 * _nbytes((k, tn), F32) + _nbytes((k, tn), BF16) \
        + 2 * _nbytes((tm, tn), out_dtype) + 2 * _nbytes((tm, tn), F32) + 6 * _nbytes((tm, LANES), F32) + (4 << 20)
    return pl.pallas_call(
        functools.partial(_norm_mm_kernel, tn=tn, n_steps=n_out // tn, rope_from=rope_from, n_rope=n_rope,
                          n_scale=n_scale, scale=scale),
        grid=(m // tm, n_out // tn),
        in_specs=[pl.BlockSpec((tm, k), lambda i, j: (i, 0)),
                  pl.BlockSpec((1, k), lambda i, j: (0, 0)),
                  pl.BlockSpec((None, k, tn), lambda i, j: (widx, 0, j)),
                  rspec, rspec, rspec],
        out_specs=pl.BlockSpec((tm, tn), lambda i, j: (i, j)),
        out_shape=jax.ShapeDtypeStruct((m, n_out), out_dtype),
        scratch_shapes=[pltpu.VMEM((tm, k), BF16)],
        compiler_params=_params(("arbitrary", "arbitrary"), vm),
        name=name,
    )(x, g.reshape(1, k), w3, *rope)


def _mm_resid_kernel(*refs, n_x):
    xs = refs[:n_x]
    ws = refs[n_x:2 * n_x]
    r_ref = refs[2 * n_x]
    o_ref = refs[2 * n_x + 1]
    acc = r_ref[...]
    for x_ref, w_ref in zip(xs, ws):
        acc = acc + _dot(x_ref[...].astype(BF16), w_ref[...].astype(BF16))
    o_ref[...] = acc


def matmul_resid(xs, w3, widx, resid, *, tm=1024, tn=1024, name="mm_resid"):
    m, kx = xs[0].shape
    n = w3.shape[2]
    n_x = len(xs)
    assert w3.shape[1] == n_x * kx and m % tm == 0 and n % tn == 0
    in_specs = [pl.BlockSpec((tm, kx), lambda j, i: (i, 0)) for _ in xs]
    in_specs += [pl.BlockSpec((None, kx, tn), lambda j, i, q=q: (widx, q, j)) for q in range(n_x)]
    in_specs += [pl.BlockSpec((tm, tn), lambda j, i: (i, j))]
    vm = n_x * (2 * _nbytes((tm, kx), xs[0].dtype) + 3 * _nbytes((kx, tn), F32)) + 6 * _nbytes((tm, tn), F32) + (4 << 20)
    return pl.pallas_call(
        functools.partial(_mm_resid_kernel, n_x=n_x),
        grid=(n // tn, m // tm),
        in_specs=in_specs,
        out_specs=pl.BlockSpec((tm, tn), lambda j, i: (i, j)),
        out_shape=jax.ShapeDtypeStruct((m, n), F32),
        compiler_params=_params(("arbitrary", "arbitrary"), vm),
        name=name,
    )(*xs, *([w3] * n_x), resid)


def _rmsnorm_kernel(x_ref, g_ref, o_ref):
    o_ref[...] = _rms(x_ref[...], g_ref[...]).astype(o_ref.dtype)


def rmsnorm(x, g, out_dtype, *, tm=512, name="rmsnorm"):
    m, k = x.shape
    tm = min(tm, m)
    assert m % tm == 0
    vm = 4 * _nbytes((tm, k), F32) + 2 * _nbytes((tm, k), out_dtype) + (4 << 20)
    return pl.pallas_call(
        _rmsnorm_kernel,
        grid=(m // tm,),
        in_specs=[pl.BlockSpec((tm, k), lambda i: (i, 0)), pl.BlockSpec((1, k), lambda i: (0, 0))],
        out_specs=pl.BlockSpec((tm, k), lambda i: (i, 0)),
        out_shape=jax.ShapeDtypeStruct((m, k), out_dtype),
        compiler_params=_params(("arbitrary",), vm),
        name=name,
    )(x, g.reshape(1, k))


def _win_attn_kernel(sink_ref, q_ref, kp_ref, kc_ref, kn_ref, vp_ref, vc_ref, vn_ref, o_ref, *, tq, seq, grp):
    i = pl.program_id(1)
    hw = A_HALF_WINDOW
    hd = HEAD_DIM
    k = jnp.concatenate([kp_ref[...], kc_ref[...], kn_ref[...]], axis=0)
    v = jnp.concatenate([vp_ref[...], vc_ref[...], vn_ref[...]], axis=0)
    wk = tq + 2 * hw
    qpos = i * tq + lax.broadcasted_iota(jnp.int32, (tq, wk), 0)
    kpos = i * tq - hw + lax.broadcasted_iota(jnp.int32, (tq, wk), 1)
    valid = (jnp.abs(kpos - qpos) <= hw) & (kpos >= 0) & (kpos < seq)
    for kv in range(A_KV_HEADS):
        kh = k[:, kv * hd:(kv + 1) * hd]
        vh = v[:, kv * hd:(kv + 1) * hd]
        for g in range(grp):
            head = kv * grp + g
            sk = sink_ref[head]
            s = _dot_nt(q_ref[:, head * hd:(head + 1) * hd], kh)
            s = jnp.where(valid, s, NEG)
            m = jnp.maximum(jnp.max(s, axis=1, keepdims=True), sk)
            p = jnp.exp(s - m)
            den = jnp.sum(p, axis=1, keepdims=True) + jnp.exp(sk - m)
            o = _dot(p.astype(BF16), vh) / den
            o_ref[:, head * hd:(head + 1) * hd] = o.astype(o_ref.dtype)


def window_attention(z, sink, bsz, seq, *, tq=256, name="win_attn"):
    t = z.shape[0]
    hw = A_HALF_WINDOW
    grp = A_Q_HEADS // A_KV_HEADS
    nq = seq // tq
    r = tq // hw
    nhb = seq // hw
    kcol = COL_KA // A_KV_W
    vcol = COL_VA // A_KV_W
    assert COL_KA % A_KV_W == 0 and COL_VA % A_KV_W == 0
    cur = lambda col: pl.BlockSpec((tq, A_KV_W), lambda b, i, s: (b * nq + i, col))
    prev = lambda col: pl.BlockSpec((hw, A_KV_W), lambda b, i, s: (b * nhb + jnp.maximum(i * r - 1, 0), col))
    nxt = lambda col: pl.BlockSpec((hw, A_KV_W), lambda b, i, s: (b * nhb + jnp.minimum((i + 1) * r, nhb - 1), col))
    grid_spec = pltpu.PrefetchScalarGridSpec(
        num_scalar_prefetch=1,
        grid=(bsz, nq),
        in_specs=[pl.BlockSpec((tq, A_Q_W), lambda b, i, s: (b * nq + i, 0)),
                  prev(kcol), cur(kcol), nxt(kcol), prev(vcol), cur(vcol), nxt(vcol)],
        out_specs=pl.BlockSpec((tq, A_Q_W), lambda b, i, s: (b * nq + i, 0)),
    )
    return pl.pallas_call(
        functools.partial(_win_attn_kernel, tq=tq, seq=seq, grp=grp),
        grid_spec=grid_spec,
        out_shape=jax.ShapeDtypeStruct((t, A_Q_W), BF16),
        compiler_params=_params(("arbitrary",) * 2, 32 << 20),
        name=name,
    )(sink.astype(F32), z, z, z, z, z, z, z)


def _dilated_bias(tq, wk, reach):
    row = lax.broadcasted_iota(jnp.int32, (tq, wk), 0)
    col = lax.broadcasted_iota(jnp.int32, (tq, wk), 1)
    d = col - reach - row
    ad = jnp.abs(d)
    mult = jnp.zeros((tq, wk), jnp.int32)
    for window, dil in C_PATTERNS:
        mult = mult + ((ad <= window // 2) & ((d & (dil - 1)) == 0)).astype(jnp.int32)
    bias = jnp.where(mult == 1, 0.0, jnp.where(mult == 2, 1.0, math.log2(3.0)))
    return jnp.where(mult == 0, NEG, bias).astype(F32)


def _dil_attn_kernel(q_ref, k_ref, v_ref, c_ref, s1_ref, s2_ref, o_ref, kp_ref, vp_ref, bias_ref, *, tq, seq, reach,
                     nh, rb, nsub):
    b = pl.program_id(0)
    hg = pl.program_id(1)
    i = pl.program_id(2)
    wk = tq + 2 * reach
    hd = HEAD_DIM

    @pl.when((b == 0) & (hg == 0) & (i == 0))
    def _():
        bias_ref[...] = _dilated_bias(tq, wk, reach)
        kp_ref[...] = jnp.zeros_like(kp_ref)
        vp_ref[...] = jnp.zeros_like(vp_ref)
        lane = lax.broadcasted_iota(jnp.int32, (seq, hd), 1)
        ones_col = jnp.where(lane == 0, 1.0, 0.0).astype(BF16)
        for h in range(nh):
            vp_ref[h, reach:reach + seq, hd:2 * hd] = ones_col

    @pl.when(i == 0)
    def _():
        for h in range(nh):
            kp_ref[h, reach:reach + seq, :] = k_ref[:, h * hd:(h + 1) * hd]
            vp_ref[h, reach:reach + seq, 0:hd] = v_ref[:, h * hd:(h + 1) * hd]

    def attend(edge):
        for w in range(nsub):
            first = (i * nsub + w) * tq
            start = pl.multiple_of(first, tq)
            if edge:
                kpos = first - reach + lax.broadcasted_iota(jnp.int32, (1, wk), 1)
                colbias = jnp.where((kpos >= 0) & (kpos < seq), 0.0, NEG).astype(F32)
            for h in range(nh):
                ks = kp_ref[h, pl.ds(start, wk), :]
                vs = vp_ref[h, pl.ds(start, wk), :]
                for r0 in range(0, tq, rb):
                    rows = slice(w * tq + r0, w * tq + r0 + rb)
                    q = _rope_tile(q_ref[rows, h * hd:(h + 1) * hd].astype(F32), c_ref[rows, :], s1_ref[rows, :],
                                   s2_ref[rows, :])
                    q = (q * Q_SCALE_LOG2).astype(BF16)
                    s = _dot_nt(q, ks) + bias_ref[r0:r0 + rb, :]
                    if edge:
                        s = s + colbias
                    m = jnp.max(s, axis=1, keepdims=True)
                    p = jnp.exp2(s - m).astype(BF16)
                    pv = _dot(p, vs)
                    o = pv[:, 0:hd] / pv[:, hd:hd + 1]
                    o_ref[w * tq + r0:w * tq + r0 + rb, h * hd:(h + 1) * hd] = o.astype(o_ref.dtype)

    is_edge = (i * nsub * tq < reach) | ((i + 1) * nsub * tq + reach > seq)
    pl.when(is_edge)(lambda: attend(True))
    pl.when(jnp.logical_not(is_edge))(lambda: attend(False))


Q_SCALE_LOG2 = HEAD_DIM ** -0.5 * math.log2(math.e)


def dilated_attention(z, rope, bsz, seq, *, tq=256, nh=2, rb=128, nsub=4, name="dil_attn"):
    t = z.shape[0]
    reach = max(w // 2 for w, _ in C_PATTERNS)
    tb = nsub * tq
    nq = seq // tb
    wk = tq + 2 * reach
    ng = C_HEADS // nh
    gw = nh * HEAD_DIM
    assert seq % tb == 0 and seq >= wk and C_HEADS % nh == 0
    vm = 4 * _nbytes((seq, gw), BF16) + 3 * nh * _nbytes((seq + 2 * reach, HEAD_DIM), BF16) \
        + (1 + 3 * nh) * _nbytes((tq, wk), F32) + (8 << 20)
    return pl.pallas_call(
        functools.partial(_dil_attn_kernel, tq=tq, seq=seq, reach=reach, nh=nh, rb=rb, nsub=nsub),
        grid=(bsz, ng, nq),
        in_specs=[pl.BlockSpec((tb, gw), lambda b, g, i: (b * nq + i, g)),
                  pl.BlockSpec((seq, gw), lambda b, g, i: (b, ng + g)),
                  pl.BlockSpec((seq, gw), lambda b, g, i: (b, 2 * ng + g))]
        + [pl.BlockSpec((tb, LANES), lambda b, g, i: (i, 0))] * 3,
        out_specs=pl.BlockSpec((tb, gw), lambda b, g, i: (b * nq + i, g)),
        out_shape=jax.ShapeDtypeStruct((t, D_MODEL), BF16),
        scratch_shapes=[pltpu.VMEM((nh, seq + 2 * reach, HEAD_DIM), BF16),
                        pltpu.VMEM((nh, seq + 2 * reach, 2 * HEAD_DIM), BF16),
                        pltpu.VMEM((tq, wk), F32)],
        compiler_params=_params(("arbitrary",) * 3, vm),
        name=name,
    )(z, z, z, *rope)


def _xattn_kernel(x_ref, g_ref, wq_ref, kv_ref, wo_ref, gn_ref, o_ref, *rest, emit_next):
    wq_s, wo_s = rest[-2:]

    @pl.when(pl.program_id(0) == 0)
    def _():
        wq_s[...] = wq_ref[...].astype(BF16)
        wo_s[...] = wo_ref[...].astype(BF16)

    x = x_ref[...]
    xn = _rms(x, g_ref[...]).astype(BF16)
    q = (_dot(xn, wq_s[...]) * (HEAD_DIM ** -0.5)).astype(BF16)
    kv = kv_ref[...]
    outs = []
    for h in range(X_HEADS):
        k = kv[:, h * HEAD_DIM:(h + 1) * HEAD_DIM]
        v = kv[:, X_W + h * HEAD_DIM:X_W + (h + 1) * HEAD_DIM]
        s = _dot_nt(q[:, h * HEAD_DIM:(h + 1) * HEAD_DIM], k)
        m = jnp.max(s, axis=1, keepdims=True)
        p = jnp.exp(s - m)
        den = jnp.sum(p, axis=1, keepdims=True)
        outs.append((_dot(p.astype(BF16), v) / den).astype(BF16))
    o = jnp.concatenate(outs, axis=1)
    y = x + _dot(o, wo_s[...])
    o_ref[...] = y
    if emit_next:
        rest[0][...] = _rms(y, gn_ref[...]).astype(BF16)


def cross_attention(x, g, wq3, wo3, layer, kv, seq, mem_len, next_gain=None, *, tm=512, name="xattn"):
    t, d = x.shape
    nsb = seq // tm
    assert seq % tm == 0
    emit_next = next_gain is not None
    vm = 4 * _nbytes((tm, d), F32) + 3 * _nbytes((d, X_W), F32) * 2 + 8 * _nbytes((tm, d), F32) + (8 << 20)
    row = pl.BlockSpec((tm, d), lambda i: (i, 0))
    vec = pl.BlockSpec((1, d), lambda i: (0, 0))
    out = pl.pallas_call(
        functools.partial(_xattn_kernel, emit_next=emit_next),
        grid=(t // tm,),
        in_specs=[row, vec,
                  pl.BlockSpec((None, d, X_W), lambda i: (layer, 0, 0)),
                  pl.BlockSpec((mem_len, 2 * X_W), lambda i: (i // nsb, 0)),
                  pl.BlockSpec((None, X_W, d), lambda i: (layer, 0, 0)),
                  vec],
        out_specs=[row, row] if emit_next else [row],
        out_shape=[jax.ShapeDtypeStruct((t, d), F32)] + ([jax.ShapeDtypeStruct((t, d), BF16)] if emit_next else []),
        scratch_shapes=[pltpu.VMEM((d, X_W), BF16), pltpu.VMEM((X_W, d), BF16)],
        compiler_params=_params(("arbitrary",), vm),
        name=name,
    )(x, g.reshape(1, d), wq3, kv, wo3, (next_gain if emit_next else g).reshape(1, d))
    return tuple(out) if emit_next else out[0]


def _conv_kernel(xp_ref, xc_ref, xn_ref, w_ref, o_ref, *, ts, ns, halo, q_scale, n_q_blocks):
    i = pl.program_id(1)
    j = pl.program_id(2)
    pad = B_CONV // 2
    xp = jnp.where(i > 0, xp_ref[halo - pad:, :].astype(F32), 0.0)
    xn = jnp.where(i < ns - 1, xn_ref[:pad, :].astype(F32), 0.0)
    xx = jnp.concatenate([xp, xc_ref[...].astype(F32), xn], axis=0)
    w = w_ref[...]
    acc = xx[0:ts, :] * w[0:1, :]
    for tap in range(1, B_CONV):
        acc = acc + xx[tap:tap + ts, :] * w[tap:tap + 1, :]
    y = acc * jax.nn.sigmoid(acc)
    y = y * jnp.where(j < n_q_blocks, q_scale, 1.0)
    o_ref[...] = y.astype(o_ref.dtype)


def conv_silu(z, conv_w8, bsz, seq, *, ts=512, tc=512, name="conv_silu"):
    t = z.shape[0]
    halo = 16
    ns = seq // ts
    nh = seq // halo
    cb = COL_QKB // tc
    assert COL_QKB % tc == 0 and seq % ts == 0 and B_W % tc == 0
    return pl.pallas_call(
        functools.partial(_conv_kernel, ts=ts, ns=ns, halo=halo, q_scale=B_HEAD_DIM ** -0.5, n_q_blocks=B_W // tc),
        grid=(bsz, ns, 2 * B_W // tc),
        in_specs=[pl.BlockSpec((halo, tc), lambda b, i, j: (b * nh + jnp.maximum(i * (ts // halo) - 1, 0), cb + j)),
                  pl.BlockSpec((ts, tc), lambda b, i, j: (b * ns + i, cb + j)),
                  pl.BlockSpec((halo, tc), lambda b, i, j: (b * nh + jnp.minimum((i + 1) * (ts // halo), nh - 1), cb + j)),
                  pl.BlockSpec((8, tc), lambda b, i, j: (0, j))],
        out_specs=pl.BlockSpec((ts, tc), lambda b, i, j: (b * ns + i, j)),
        out_shape=jax.ShapeDtypeStruct((t, 2 * B_W), BF16),
        compiler_params=_params(("arbitrary",) * 3, 32 << 20),
        name=name,
    )(z, z, z, conv_w8)


def _log_sigmoid(x):
    return jnp.minimum(x, 0.0) - jnp.log(1.0 + jnp.exp(-jnp.abs(x)))


def _mlstm_chunk(q, k, v, i_col, f_col, i_row, f_row, c_ref, n_ref, m_ref, rev):
    L = q.shape[0]
    logf_c = _log_sigmoid(f_col)
    logf_r = _log_sigmoid(f_row)
    row = lax.broadcasted_iota(jnp.int32, (L, L), 0)
    col = lax.broadcasted_iota(jnp.int32, (L, L), 1)
    causal = (col >= row) if rev else (col <= row)
    causal_t = (row >= col) if rev else (row <= col)
    b_col = jnp.sum(jnp.where(causal, logf_r, 0.0), axis=1, keepdims=True)
    b_row = jnp.sum(jnp.where(causal_t, logf_c, 0.0), axis=0, keepdims=True)
    b_all = jnp.sum(logf_r, axis=1, keepdims=True)
    m_st = m_ref[...]
    c_st = c_ref[...]
    n_st = n_ref[...]
    logd = jnp.where(causal, b_col - b_row + i_row, NEG)
    m_inter = b_col + m_st
    mt = jnp.maximum(m_inter, jnp.max(logd, axis=1, keepdims=True))
    s = _dot_nt(q, k) * jnp.exp(logd - mt)
    sc = jnp.exp(m_inter - mt)
    num = _dot(s.astype(BF16), v) + sc * _dot(q, c_st.astype(BF16))
    den = jnp.sum(s, axis=1, keepdims=True) + sc * jnp.sum(q.astype(F32) * n_st, axis=1, keepdims=True)
    h = num / jnp.maximum(jnp.abs(den), jnp.exp(-mt))
    logw = b_all - b_col + i_col
    m_new = jnp.maximum(b_all + m_st, jnp.max(logw, axis=0, keepdims=True))
    wgt = jnp.exp(logw - m_new)
    dec = jnp.exp(b_all + m_st - m_new)
    kw = k.astype(F32) * wgt
    c_ref[...] = dec * c_st + _dot_tn(kw.astype(BF16), v)
    n_ref[...] = dec * n_st + jnp.sum(kw, axis=0, keepdims=True)
    m_ref[...] = m_new
    return h


def _mlstm_kernel(bias_ref, qf_ref, kf_ref, vf0_ref, vf1_ref, gcf_ref, grf_ref,
                  qb_ref, kb_ref, vb0_ref, vb1_ref, gcb_ref, grb_ref,
                  hf_ref, hb_ref, c_ref, n_ref, m_ref):
    @pl.when(pl.program_id(1) == 0)
    def _():
        c_ref[...] = jnp.zeros_like(c_ref)
        n_ref[...] = jnp.zeros_like(n_ref)
        m_ref[...] = jnp.zeros_like(m_ref)

    lane = lax.broadcasted_iota(jnp.int32, (1, LANES), 1)
    dh = B_HEAD_DIM
    for d, (q_ref, k_ref, v0_ref, v1_ref, gc_ref, gr_ref, h_ref) in enumerate((
            (qf_ref, kf_ref, vf0_ref, vf1_ref, gcf_ref, grf_ref, hf_ref),
            (qb_ref, kb_ref, vb0_ref, vb1_ref, gcb_ref, grb_ref, hb_ref))):
        gc = gc_ref[...]
        for hd in range(B_HEADS):
            ci = 2 * d * B_HEADS + hd
            cf = ci + B_HEADS
            i_col = jnp.sum(jnp.where(lane == ci, gc, 0.0), axis=1, keepdims=True) + bias_ref[ci]
            f_col = jnp.sum(jnp.where(lane == cf, gc, 0.0), axis=1, keepdims=True) + bias_ref[cf]
            i_row = gr_ref[ci:ci + 1, :] + bias_ref[ci]
            f_row = gr_ref[cf:cf + 1, :] + bias_ref[cf]
            v_ref = v0_ref if hd < B_HEADS // 2 else v1_ref
            vo = (hd % (B_HEADS // 2)) * dh
            st = d * B_HEADS + hd
            h = _mlstm_chunk(q_ref[:, hd * dh:(hd + 1) * dh], k_ref[:, hd * dh:(hd + 1) * dh],
                             v_ref[:, vo:vo + dh], i_col, f_col, i_row, f_row,
                             c_ref.at[st], n_ref.at[st], m_ref.at[st], rev=bool(d))
            h_ref[:, hd * dh:(hd + 1) * dh] = h


def mlstm(qk, z, g_cols, g_rows, b_gate, bsz, seq, *, chunk=256, name="mlstm"):
    t = qk.shape[0]
    nc = seq // chunk
    assert seq % chunk == 0
    half = B_W // 2
    vcol = COL_VB // half
    assert COL_VB % half == 0
    fw = lambda b, c, s: b * nc + c
    bw = lambda b, c, s: b * nc + (nc - 1 - c)

    def specs(rowf):
        return [pl.BlockSpec((chunk, B_W), lambda b, c, s: (rowf(b, c, s), 0)),
                pl.BlockSpec((chunk, B_W), lambda b, c, s: (rowf(b, c, s), 1)),
                pl.BlockSpec((chunk, half), lambda b, c, s: (rowf(b, c, s), vcol)),
                pl.BlockSpec((chunk, half), lambda b, c, s: (rowf(b, c, s), vcol + 1)),
                pl.BlockSpec((chunk, LANES), lambda b, c, s: (rowf(b, c, s), 0)),
                pl.BlockSpec((16, chunk), lambda b, c, s: (0, rowf(b, c, s)))]

    grid_spec = pltpu.PrefetchScalarGridSpec(
        num_scalar_prefetch=1,
        grid=(bsz, nc),
        in_specs=specs(fw) + specs(bw),
        out_specs=[pl.BlockSpec((chunk, B_W), lambda b, c, s: (fw(b, c, s), 0)),
                   pl.BlockSpec((chunk, B_W), lambda b, c, s: (bw(b, c, s), 0))],
        scratch_shapes=[pltpu.VMEM((2 * B_HEADS, B_HEAD_DIM, B_HEAD_DIM), F32),
                        pltpu.VMEM((2 * B_HEADS, 1, B_HEAD_DIM), F32),
                        pltpu.VMEM((2 * B_HEADS, 1, 1), F32)],
    )
    args = (qk, qk, z, z, g_cols, g_rows)
    return pl.pallas_call(
        _mlstm_kernel,
        grid_spec=grid_spec,
        out_shape=[jax.ShapeDtypeStruct((t, B_W), F32), jax.ShapeDtypeStruct((t, B_W), F32)],
        compiler_params=_params(("arbitrary", "arbitrary"), 48 << 20),
        name=name,
    )(b_gate.astype(F32), *args, *args)


def _mlstm_out_kernel(hf_ref, hb_ref, o0_ref, o1_ref, gain_ref, y_ref):
    dh = B_HEAD_DIM
    for hd in range(B_HEADS):
        sl = slice(hd * dh, (hd + 1) * dh)
        h = hf_ref[:, sl] + hb_ref[:, sl]
        hc = h - jnp.mean(h, axis=1, keepdims=True)
        y = hc * lax.rsqrt(jnp.mean(hc * hc, axis=1, keepdims=True) + EPS) * gain_ref[:, sl]
        o_ref = o0_ref if hd < B_HEADS // 2 else o1_ref
        oo = (hd % (B_HEADS // 2)) * dh
        y_ref[:, sl] = (jax.nn.sigmoid(o_ref[:, oo:oo + dh].astype(F32)) * y).astype(y_ref.dtype)


def mlstm_output(hf, hb, z, gain, *, tm=512, name="mlstm_out"):
    t = hf.shape[0]
    half = B_W // 2
    ocol = COL_OB // half
    assert COL_OB % half == 0 and t % tm == 0
    return pl.pallas_call(
        _mlstm_out_kernel,
        grid=(t // tm,),
        in_specs=[pl.BlockSpec((tm, B_W), lambda i: (i, 0)),
                  pl.BlockSpec((tm, B_W), lambda i: (i, 0)),
                  pl.BlockSpec((tm, half), lambda i: (i, ocol)),
                  pl.BlockSpec((tm, half), lambda i: (i, ocol + 1)),
                  pl.BlockSpec((1, B_W), lambda i: (0, 0))],
        out_specs=pl.BlockSpec((tm, B_W), lambda i: (i, 0)),
        out_shape=jax.ShapeDtypeStruct((t, B_W), BF16),
        compiler_params=_params(("arbitrary",), 32 << 20),
        name=name,
    )(hf, hb, z, z, gain.reshape(1, B_W))


def _tile_state(na_ref, th_ref):
    i = pl.program_id(1)
    active = i < na_ref[0]
    half_only = th_ref[jnp.minimum(i, na_ref[0] - 1)] == 1
    return active, half_only


def _up_kernel(te_ref, na_ref, th_ref, x_ref, wg_ref, wu_ref, h_ref, *, tm):
    active, half_only = _tile_state(na_ref, th_ref)

    def compute(rows):
        x = x_ref[0:rows, :]
        g = _dot(x, wg_ref[...].astype(BF16))
        u = _dot(x, wu_ref[...].astype(BF16))
        h_ref[0:rows, :] = (g * jax.nn.sigmoid(g) * u).astype(h_ref.dtype)
        if rows < tm:
            h_ref[rows:, :] = jnp.zeros((tm - rows, h_ref.shape[1]), h_ref.dtype)

    pl.when(active & jnp.logical_not(half_only))(lambda: compute(tm))
    pl.when(active & half_only)(lambda: compute(tm // 2))

    @pl.when(jnp.logical_not(active))
    def _():
        h_ref[...] = jnp.zeros_like(h_ref)


def swiglu_up(x, w_gu3, tile_expert, n_active, tile_half, *, tm=512, tf=512, name="ffn_up"):
    p, d = x.shape
    f = w_gu3.shape[2] // 2
    nt = p // tm
    nj = f // tf
    assert p % tm == 0 and f % tf == 0
    row = lambda i, na: jnp.minimum(i, na[0] - 1)
    grid_spec = pltpu.PrefetchScalarGridSpec(
        num_scalar_prefetch=3,
        grid=(nj, nt),
        in_specs=[pl.BlockSpec((tm, d), lambda j, i, te, na, th: (row(i, na), 0)),
                  pl.BlockSpec((None, d, tf), lambda j, i, te, na, th: (te[row(i, na)], 0, j)),
                  pl.BlockSpec((None, d, tf), lambda j, i, te, na, th: (te[row(i, na)], 0, nj + j))],
        out_specs=pl.BlockSpec((tm, tf), lambda j, i, te, na, th: (i, j)),
    )
    vm = 2 * _nbytes((tm, d), BF16) + 2 * 3 * _nbytes((d, tf), F32) + 8 * _nbytes((tm, tf), F32) + (4 << 20)
    return pl.pallas_call(
        functools.partial(_up_kernel, tm=tm),
        grid_spec=grid_spec,
        out_shape=jax.ShapeDtypeStruct((p, f), BF16),
        compiler_params=_params(("arbitrary", "arbitrary"), vm),
        name=name,
    )(tile_expert, n_active, tile_half, x, w_gu3, w_gu3)


def _down_kernel(te_ref, na_ref, th_ref, h_ref, w_ref, *rest, tm, has_resid):
    o_ref = rest[-1]
    active, half_only = _tile_state(na_ref, th_ref)

    def compute(rows):
        o = _dot(h_ref[0:rows, :], w_ref[...].astype(BF16))
        if has_resid:
            o = o + rest[0][0:rows, :]
        o_ref[0:rows, :] = o
        if rows < tm:
            o_ref[rows:, :] = jnp.zeros((tm - rows, o_ref.shape[1]), o_ref.dtype)

    pl.when(active & jnp.logical_not(half_only))(lambda: compute(tm))
    pl.when(active & half_only)(lambda: compute(tm // 2))

    @pl.when(jnp.logical_not(active))
    def _():
        o_ref[...] = jnp.zeros_like(o_ref)


def swiglu_down(h, w_d3, tile_expert, n_active, tile_half, resid=None, *, tm=512, tn=512, name="ffn_down"):
    p, f = h.shape
    d = w_d3.shape[2]
    nt = p // tm
    assert p % tm == 0 and d % tn == 0
    row = lambda i, na: jnp.minimum(i, na[0] - 1)
    in_specs = [pl.BlockSpec((tm, f), lambda n, i, te, na, th: (row(i, na), 0)),
                pl.BlockSpec((None, f, tn), lambda n, i, te, na, th: (te[row(i, na)], 0, n))]
    args = [h, w_d3]
    if resid is not None:
        in_specs.append(pl.BlockSpec((tm, tn), lambda n, i, te, na, th: (row(i, na), n)))
        args.append(resid)
    grid_spec = pltpu.PrefetchScalarGridSpec(
        num_scalar_prefetch=3,
        grid=(d // tn, nt),
        in_specs=in_specs,
        out_specs=pl.BlockSpec((tm, tn), lambda n, i, te, na, th: (i, n)),
    )
    vm = 2 * _nbytes((tm, f), BF16) + 2 * _nbytes((f, tn), F32) + _nbytes((f, tn), BF16) + 8 * _nbytes((tm, tn), F32) + (4 << 20)
    return pl.pallas_call(
        functools.partial(_down_kernel, tm=tm, has_resid=resid is not None),
        grid_spec=grid_spec,
        out_shape=jax.ShapeDtypeStruct((p, d), F32),
        compiler_params=_params(("arbitrary", "arbitrary"), vm),
        name=name,
    )(tile_expert, n_active, tile_half, *args)


def _router_kernel(x_ref, g_ref, w_ref, mi_ref, mf_ref, cnt_ref, carry_ref, *, tm):
    @pl.when(pl.program_id(0) == 0)
    def _():
        carry_ref[...] = jnp.zeros_like(carry_ref)

    u = _rms(x_ref[...], g_ref[...])
    w = w_ref[...]
    u_hi = u.astype(BF16)
    w_hi = w.astype(BF16)
    u_lo = (u - u_hi.astype(F32)).astype(BF16)
    w_lo = (w - w_hi.astype(F32)).astype(BF16)
    logits = _dot(u_hi, w_hi) + (_dot(u_hi, w_lo) + _dot(u_lo, w_hi))
    lane = lax.broadcasted_iota(jnp.int32, (tm, LANES), 1)
    logits = jnp.where(lane < N_EXPERTS, logits, -jnp.inf)
    v1 = jnp.max(logits, axis=1, keepdims=True)
    i1 = jnp.min(jnp.where(logits == v1, lane, LANES), axis=1, keepdims=True)
    oh1 = lane == i1
    rest = jnp.where(oh1, -jnp.inf, logits)
    v2 = jnp.max(rest, axis=1, keepdims=True)
    i2 = jnp.min(jnp.where(rest == v2, lane, LANES), axis=1, keepdims=True)
    oh2 = lane == i2
    e = jnp.exp(v2 - v1)
    g1 = 1.0 / (1.0 + e)
    g2 = e * g1
    cnt = (oh1 | oh2).astype(F32)
    r = lax.broadcasted_iota(jnp.int32, (tm, tm), 0)
    c = lax.broadcasted_iota(jnp.int32, (tm, tm), 1)
    before = (c < r).astype(BF16)
    excl = _dot(before, cnt.astype(BF16)) + carry_ref[0:1, :]
    r1 = jnp.sum(jnp.where(oh1, excl, 0.0), axis=1, keepdims=True).astype(jnp.int32)
    r2 = jnp.sum(jnp.where(oh2, excl, 0.0), axis=1, keepdims=True).astype(jnp.int32)
    mi_ref[...] = jnp.where(lane == 0, i1, jnp.where(lane == 1, i2, jnp.where(lane == 2, r1, jnp.where(lane == 3, r2, 0))))
    mf_ref[...] = jnp.where(lane == 0, g1, jnp.where(lane == 1, g2, 0.0))
    carry_ref[...] = carry_ref[...] + jnp.sum(cnt, axis=0, keepdims=True)
    cnt_ref[...] = carry_ref[...]


def moe_router(x, g, w_router3, layer, *, tm=512, name="moe_router"):
    t, d = x.shape
    assert t % tm == 0
    vm = 6 * _nbytes((tm, d), F32) + 4 * _nbytes((d, LANES), F32) + 8 * _nbytes((tm, tm), F32) + (4 << 20)
    return pl.pallas_call(
        functools.partial(_router_kernel, tm=tm),
        grid=(t // tm,),
        in_specs=[pl.BlockSpec((tm, d), lambda i: (i, 0)),
                  pl.BlockSpec((1, d), lambda i: (0, 0)),
                  pl.BlockSpec((None, d, LANES), lambda i: (layer, 0, 0))],
        out_specs=[pl.BlockSpec((tm, LANES), lambda i: (i, 0)),
                   pl.BlockSpec((tm, LANES), lambda i: (i, 0)),
                   pl.BlockSpec((8, LANES), lambda i: (0, 0))],
        out_shape=[jax.ShapeDtypeStruct((t, LANES), jnp.int32),
                   jax.ShapeDtypeStruct((t, LANES), F32),
                   jax.ShapeDtypeStruct((8, LANES), F32)],
        scratch_shapes=[pltpu.VMEM((8, LANES), F32)],
        compiler_params=_params(("arbitrary",), vm),
        name=name,
    )(x, g.reshape(1, d), w_router3)


def _row_copy(src_hbm, dst_vmem, sem, src_row, dst_row):
    return pltpu.make_async_copy(src_hbm.at[pl.ds(src_row, 1), :], dst_vmem.at[pl.ds(dst_row, 1), :], sem)


GATHER_UNROLL = 8


def _start_rows(src_hbm, idx_refs, dst_refs, sem, n):
    for r in range(n):
        for idx_ref, dst in zip(idx_refs, dst_refs):
            _row_copy(src_hbm, dst, sem, idx_ref[0, 0, r], r).start(priority=r % 2)


def _wait_rows(src_hbm, dst_refs, sem, n):
    def body(r, carry):
        for dst in dst_refs:
            _row_copy(src_hbm, dst, sem, 0, r).wait()
        return carry

    lax.fori_loop(0, n, body, 0, unroll=GATHER_UNROLL)


def _dispatch_kernel(tok_ref, tok_next_ref, x_hbm, g_ref, o_ref, rows_ref, sems, *, tg):
    i = pl.program_id(0)
    slot = i % 2

    @pl.when(i == 0)
    def _():
        _start_rows(x_hbm, [tok_ref], [rows_ref.at[0]], sems.at[0], tg)

    @pl.when(i + 1 < pl.num_programs(0))
    def _():
        _start_rows(x_hbm, [tok_next_ref], [rows_ref.at[1 - slot]], sems.at[1 - slot], tg)

    _wait_rows(x_hbm, [rows_ref.at[slot]], sems.at[slot], tg)
    o_ref[...] = _rms(rows_ref[slot], g_ref[...]).astype(o_ref.dtype)


def moe_dispatch(x, g, token_of, *, tg=256, name="moe_dispatch"):
    t, d = x.shape
    p = token_of.shape[0]
    assert p % tg == 0
    nt = p // tg
    vm = 4 * _nbytes((tg, d), F32) + 2 * _nbytes((tg, d), BF16) + (4 << 20)
    tok3 = token_of.reshape(nt, 1, tg)
    return pl.pallas_call(
        functools.partial(_dispatch_kernel, tg=tg),
        grid=(nt,),
        in_specs=[pl.BlockSpec((1, 1, tg), lambda i: (i, 0, 0), memory_space=pltpu.SMEM),
                  pl.BlockSpec((1, 1, tg), lambda i: (jnp.minimum(i + 1, nt - 1), 0, 0), memory_space=pltpu.SMEM),
                  pl.BlockSpec(memory_space=pl.ANY),
                  pl.BlockSpec((1, d), lambda i: (0, 0))],
        out_specs=pl.BlockSpec((tg, d), lambda i: (i, 0)),
        out_shape=jax.ShapeDtypeStruct((p, d), BF16),
        scratch_shapes=[pltpu.VMEM((2, tg, d), F32), pltpu.SemaphoreType.DMA((2,))],
        compiler_params=_params(("arbitrary",), vm),
        name=name,
    )(tok3, tok3, x, g.reshape(1, d))


def _combine_kernel(p1_ref, p2_ref, p1n_ref, p2n_ref, o_hbm, mf_ref, x_ref, gf_ref, y_ref, rows_ref, sems, *, tc,
                    final_norm):
    i = pl.program_id(0)
    slot = i % 2
    dst = lambda s: [rows_ref.at[s, 0], rows_ref.at[s, 1]]

    @pl.when(i == 0)
    def _():
        _start_rows(o_hbm, [p1_ref, p2_ref], dst(0), sems.at[0], tc)

    @pl.when(i + 1 < pl.num_programs(0))
    def _():
        _start_rows(o_hbm, [p1n_ref, p2n_ref], dst(1 - slot), sems.at[1 - slot], tc)

    _wait_rows(o_hbm, dst(slot), sems.at[slot], tc)
    mf = mf_ref[...]
    y = x_ref[...] + mf[:, 0:1] * rows_ref[slot, 0] + mf[:, 1:2] * rows_ref[slot, 1]
    y_ref[...] = _rms(y, gf_ref[...]) if final_norm else y


def moe_combine(o, pos1, pos2, gates, x, final_gain=None, *, tc=256, name="moe_combine"):
    t, d = x.shape
    final_norm = final_gain is not None
    assert t % tc == 0
    nt = t // tc
    vm = 4 * _nbytes((tc, d), F32) + 6 * _nbytes((tc, d), F32) + (4 << 20)
    cur = pl.BlockSpec((1, 1, tc), lambda i: (i, 0, 0), memory_space=pltpu.SMEM)
    nxt = pl.BlockSpec((1, 1, tc), lambda i: (jnp.minimum(i + 1, nt - 1), 0, 0), memory_space=pltpu.SMEM)
    p1 = pos1.reshape(nt, 1, tc)
    p2 = pos2.reshape(nt, 1, tc)
    return pl.pallas_call(
        functools.partial(_combine_kernel, tc=tc, final_norm=final_norm),
        grid=(nt,),
        in_specs=[cur, cur, nxt, nxt,
                  pl.BlockSpec(memory_space=pl.ANY),
                  pl.BlockSpec((tc, LANES), lambda i: (i, 0)),
                  pl.BlockSpec((tc, d), lambda i: (i, 0)),
                  pl.BlockSpec((1, d), lambda i: (0, 0))],
        out_specs=pl.BlockSpec((tc, d), lambda i: (i, 0)),
        out_shape=jax.ShapeDtypeStruct((t, d), F32),
        scratch_shapes=[pltpu.VMEM((2, 2, tc, d), F32), pltpu.SemaphoreType.DMA((2,))],
        compiler_params=_params(("arbitrary",), vm),
        name=name,
    )(p1, p2, p1, p2, o, gates, x, (final_gain if final_norm else jnp.ones((d,), F32)).reshape(1, d))


def _rope_tables(seq):
    pos = jnp.arange(seq, dtype=F32)
    inv = jnp.power(ROPE_THETA, -jnp.arange(0, ROT_DIM, 2, dtype=F32) / ROT_DIM)
    ang = pos[:, None] * inv[None, :]
    cos, sin = jnp.cos(ang), jnp.sin(ang)
    half = ROT_DIM // 2
    zeros = jnp.zeros((seq, LANES - ROT_DIM), F32)
    c = jnp.concatenate([cos, cos, jnp.ones_like(zeros)], axis=1)
    s1 = jnp.concatenate([-sin, jnp.zeros((seq, half), F32), zeros], axis=1)
    s2 = jnp.concatenate([jnp.zeros((seq, half), F32), sin, zeros], axis=1)
    return c, s1, s2


def _moe_plan(meta_i, counts, t, tm):
    eid = meta_i[:, 0:2]
    rank = meta_i[:, 2:4]
    cnt = counts[0, :N_EXPERTS].astype(jnp.int32)
    padded = ((cnt + tm - 1) // tm) * tm
    ends = jnp.cumsum(padded)
    starts = ends - padded
    pos = starts[eid] + rank
    p_max = 2 * t + N_EXPERTS * tm
    nt = p_max // tm
    n_active = jnp.maximum(ends[-1] // tm, 1)
    tile_start = jnp.minimum(jnp.arange(nt, dtype=jnp.int32), n_active - 1) * tm
    owner = jnp.sum((tile_start[:, None] >= ends[None, :]).astype(jnp.int32), axis=1)
    tile_expert = jnp.minimum(owner, N_EXPERTS - 1)
    tok = jnp.broadcast_to(jnp.arange(t, dtype=jnp.int32)[:, None], (t, 2))
    token_of = jnp.zeros((p_max,), jnp.int32).at[pos.reshape(-1)].set(tok.reshape(-1))
    valid = jnp.clip((starts + cnt)[tile_expert] - tile_start, 0, tm)
    tile_half = (valid <= tm // 2).astype(jnp.int32)
    return pos[:, 0], pos[:, 1], token_of, tile_expert, n_active.reshape(1).astype(jnp.int32), tile_half


FFN_TILE = 512
MOE_UP_COLS = 1024
DENSE_UP_TILE = 1024


def even_mixer(xs, g, w_in3, gate_w3, conv_w8, b_gate, sink, head_gain, w_out3, j, rope, bsz, seq):
    z = norm_matmul(xs, g, w_in3, j, EVEN_MAIN, rope=rope, seq=seq, n_rope=(A_Q_W + A_KV_W) // HEAD_DIM,
                    n_scale=A_Q_W // HEAD_DIM, scale=HEAD_DIM ** -0.5, name="even_in")
    g_cols = norm_matmul(xs, g, gate_w3, j, LANES, out_dtype=F32, name="even_gates")
    g_rows = g_cols[:, :4 * B_HEADS].T
    ya = window_attention(z, sink, bsz, seq)
    qk = conv_silu(z, conv_w8, bsz, seq)
    hf, hb = mlstm(qk, z, g_cols, g_rows, b_gate, bsz, seq)
    yb = mlstm_output(hf, hb, z, head_gain)
    return matmul_resid([ya, yb], w_out3, j, xs, name="even_out")


def odd_mixer(xs, g, w_in3, w_out3, j, rope, bsz, seq):
    d = xs.shape[1]
    z = norm_matmul(xs, g, w_in3, j, 3 * d, rope=rope, seq=seq, rope_from=C_HEADS, n_rope=2 * C_HEADS, name="odd_in")
    y = dilated_attention(z, rope, bsz, seq)
    return matmul_resid([y], w_out3, j, xs, name="odd_out")


def dense_ffn(xs, u, w_gu3, w_d3, j):
    t = xs.shape[0]
    plan = lambda tm: (jnp.full((t // tm,), j, jnp.int32), jnp.full((1,), t // tm, jnp.int32),
                       jnp.zeros((t // tm,), jnp.int32))
    h = swiglu_up(u, w_gu3, *plan(DENSE_UP_TILE), tm=DENSE_UP_TILE, name="ffn_up")
    return swiglu_down(h, w_d3, *plan(FFN_TILE), resid=xs, tm=FFN_TILE, name="ffn_down")


def moe_ffn(xs, g, router_w3, w_gu3, w_d3, j, final_gain=None):
    t = xs.shape[0]
    meta_i, gates, counts = moe_router(xs, g, router_w3, j)
    pos1, pos2, token_of, tile_expert, n_active, tile_half = _moe_plan(meta_i, counts, t, FFN_TILE)
    xd = moe_dispatch(xs, g, token_of)
    experts = tile_expert + j * N_EXPERTS
    h = swiglu_up(xd, w_gu3, experts, n_active, tile_half, tm=FFN_TILE, tf=MOE_UP_COLS, name="moe_up")
    o = swiglu_down(h, w_d3, experts, n_active, tile_half, tm=FFN_TILE, name="moe_down")
    return moe_combine(o, pos1, pos2, gates, xs, final_gain)


def kernel(x, mem, ln_mix, ln_xattn, ln_mem, ln_ffn, ln_final, ev_w_in, ev_b_gate, ev_conv, ev_sink, ev_head_norm,
           ev_w_out, ffn_w_gu, ffn_w_down, od_w_in, od_w_out, moe_router, moe_w_gu, moe_w_down, x_wq, x_wkv, x_wo):
    bsz, seq, d = x.shape
    mem_len = mem.shape[1]
    depth = ln_mix.shape[0]
    t = bsz * seq
    rope = _rope_tables(seq)
    xs = x.reshape(t, d)
    memf = mem.reshape(bsz * mem_len, d)
    n_odd = od_w_in.shape[0]
    moe_gu = moe_w_gu.reshape(n_odd * N_EXPERTS, d, 2 * FFN_DIM)
    moe_dn = moe_w_down.reshape(n_odd * N_EXPERTS, FFN_DIM, d)
    router_w = jnp.pad(moe_router, ((0, 0), (0, 0), (0, LANES - N_EXPERTS)))
    gate_w = jnp.pad(ev_w_in[:, :, EVEN_MAIN:], ((0, 0), (0, 0), (0, LANES - 4 * B_HEADS)))
    conv_w8 = jnp.pad(ev_conv, ((0, 0), (0, 8 - B_CONV), (0, 0)))

    for layer in range(depth):
        j = layer // 2
        if layer % 2 == 0:
            xs = even_mixer(xs, ln_mix[layer], ev_w_in, gate_w, conv_w8[j], ev_b_gate[j], ev_sink[j],
                            ev_head_norm[j], ev_w_out, j, rope, bsz, seq)
        else:
            xs = odd_mixer(xs, ln_mix[layer], od_w_in, od_w_out, j, rope, bsz, seq)
        kv = norm_matmul(memf, ln_mem[layer], x_wkv, layer, 2 * X_W, name="mem_kv")
        last = layer == depth - 1
        if layer % 2 == 0:
            xs, u = cross_attention(xs, ln_xattn[layer], x_wq, x_wo, layer, kv, seq, mem_len, ln_ffn[layer])
            xs = dense_ffn(xs, u, ffn_w_gu, ffn_w_down, j)
            if last:
                xs = rmsnorm(xs, ln_final, F32, name="final_norm")
        else:
            xs = cross_attention(xs, ln_xattn[layer], x_wq, x_wo, layer, kv, seq, mem_len)
            xs = moe_ffn(xs, ln_ffn[layer], router_w, moe_gu, moe_dn, j, ln_final if last else None)
    return xs.reshape(bsz, seq, d)
```
